```python
import math
import jax, jax.numpy as jnp
from jax import lax
import numpy as np

D_MODEL = 4096
BATCH = 2
SEQ = 8192
DEPTH = 1

CHUNK = 64
HEAD_DIM = 128
N_HEADS_A = 16
N_HEADS_B = 16
D_A = N_HEADS_A * HEAD_DIM
D_B = N_HEADS_B * HEAD_DIM
N_IDX_HEADS = 32
IDX_DIM = 64
TOPK_MAX = 256
N_BUCKETS = 32
MAX_DISTANCE = 1024
D_FF = 4 * D_MODEL
Q_BLOCK = 128
EPS = 1e-6
SPLIT_SIZES = (D_A, D_A, D_A, N_IDX_HEADS * IDX_DIM, IDX_DIM, N_IDX_HEADS,
               D_B, D_B, D_B, N_HEADS_B, D_MODEL, D_MODEL)
D_IN = sum(SPLIT_SIZES)

kernel_name = 'hybrid_dsa_fox_gated_block'


def rmsnorm(x, g):
    x32 = x.astype(jnp.float32)
    y = x32 * lax.rsqrt(jnp.mean(x32 * x32, axis=-1, keepdims=True) + EPS)
    return y.astype(x.dtype) * g


def t5_bucket(rel):
    half = N_BUCKETS // 2
    max_exact = half // 2
    base = jnp.where(rel > 0, half, 0)
    n = jnp.abs(rel)
    nf = jnp.maximum(n, max_exact).astype(jnp.float32)
    large = max_exact + (jnp.log(nf / max_exact) / math.log(MAX_DISTANCE / max_exact)
                         * (half - max_exact)).astype(jnp.int32)
    large = jnp.minimum(large, half - 1)
    return base + jnp.where(n < max_exact, n, large)


def dsa_attention(q, k, v, iq, ik, iw, rel_bias):
    B, S = q.shape[0], q.shape[1]
    n_blocks = S // CHUNK
    top_k = min(TOPK_MAX, S // 4)
    key_pos = jnp.arange(S, dtype=jnp.int32)
    b_idx = jnp.arange(B)[:, None, None]
    scale = HEAD_DIM ** -0.5

    def block(c):
        start = c * CHUNK
        qb = lax.dynamic_slice_in_dim(q, start, CHUNK, axis=1)
        iqb = lax.dynamic_slice_in_dim(iq, start, CHUNK, axis=1)
        iwb = lax.dynamic_slice_in_dim(iw, start, CHUNK, axis=1)
        sc = jnp.einsum('bthd,bsd->bths', iqb, ik) * (IDX_DIM ** -0.5)
        score = jnp.einsum('bths,bth->bts', jax.nn.relu(sc), iwb).astype(jnp.float32)
        limit = start + CHUNK
        score = jnp.where(key_pos[None, None, :] < limit, score, -jnp.inf)
        _, idx = lax.top_k(score, top_k)
        valid = idx < limit
        kg = k[b_idx, idx]
        vg = v[b_idx, idx]
        logits = jnp.einsum('bthd,btkhd->bhtk', qb, kg).astype(jnp.float32) * scale
        q_pos = start + jnp.arange(CHUNK, dtype=jnp.int32)
        bias = rel_bias[t5_bucket(idx - q_pos[None, :, None])]
        logits = logits + jnp.transpose(bias, (0, 3, 1, 2)).astype(jnp.float32)
        logits = jnp.where(valid[:, None, :, :], logits, -jnp.inf)
        p = jax.nn.softmax(logits, axis=-1)
        return jnp.einsum('bhtk,btkhd->bthd', p.astype(v.dtype), vg)

    out = lax.map(block, jnp.arange(n_blocks))
    out = jnp.transpose(out, (1, 0, 2, 3, 4))
    return out.reshape(B, S, N_HEADS_A * HEAD_DIM)


def fox_attention(q, k, v, log_f):
    B, S = q.shape[0], q.shape[1]
    n_blocks = S // Q_BLOCK
    cum = jnp.transpose(lax.cumsum(log_f, axis=1), (0, 2, 1))
    key_pos = jnp.arange(S, dtype=jnp.int32)
    scale = HEAD_DIM ** -0.5

    def block(i):
        start = i * Q_BLOCK
        qb = lax.dynamic_slice_in_dim(q, start, Q_BLOCK, axis=1)
        cq = lax.dynamic_slice_in_dim(cum, start, Q_BLOCK, axis=2)
        logits = jnp.einsum('bthd,bshd->bhts', qb, k).astype(jnp.float32) * scale
        logits = logits + cq[:, :, :, None] - cum[:, :, None, :]
        q_pos = start + jnp.arange(Q_BLOCK, dtype=jnp.int32)
        logits = jnp.where(key_pos[None, :] <= q_pos[:, None], logits, -jnp.inf)
        p = jax.nn.softmax(logits, axis=-1)
        return jnp.einsum('bhts,bshd->bthd', p.astype(v.dtype), v)

    out = lax.map(block, jnp.arange(n_blocks))
    out = jnp.transpose(out, (1, 0, 2, 3, 4))
    return out.reshape(B, S, N_HEADS_B * HEAD_DIM)


def setup_inputs(seed: int = 0) -> dict:
    key = jax.random.key(seed)
    ks = jax.random.split(key, 12)
    f32 = jnp.float32
    x = jax.random.normal(ks[0], (BATCH, SEQ, D_MODEL), f32)
    w_in = jax.random.normal(ks[1], (DEPTH, D_MODEL, D_IN), f32) * D_MODEL ** -0.5
    b_f = 2.0 + 0.5 * jax.random.normal(ks[2], (DEPTH, N_HEADS_B), f32)
    w_out_a = jax.random.normal(ks[3], (DEPTH, D_A, D_MODEL), f32) * D_A ** -0.5
    w_out_b = jax.random.normal(ks[4], (DEPTH, D_B, D_MODEL), f32) * D_B ** -0.5
    w_o = jax.random.normal(ks[5], (DEPTH, D_MODEL, D_MODEL), f32) * D_MODEL ** -0.5
    w_up = jax.random.normal(ks[6], (DEPTH, D_MODEL, D_FF), f32) * D_MODEL ** -0.5
    w_down = jax.random.normal(ks[7], (DEPTH, D_FF, D_MODEL), f32) * D_FF ** -0.5
    g_mix = 1.0 + 0.01 * jax.random.normal(ks[8], (DEPTH, D_MODEL), f32)
    g_mlp = 1.0 + 0.01 * jax.random.normal(ks[9], (DEPTH, D_MODEL), f32)
    g_final = 1.0 + 0.01 * jax.random.normal(ks[10], (D_MODEL,), f32)
    rel_bias = 0.5 * jax.random.normal(ks[11], (N_BUCKETS, N_HEADS_A), f32)
    return {'x': x, 'w_in': w_in, 'b_f': b_f, 'w_out_a': w_out_a, 'w_out_b': w_out_b,
            'w_o': w_o, 'w_up': w_up, 'w_down': w_down, 'g_mix': g_mix, 'g_mlp': g_mlp,
            'g_final': g_final, 'rel_bias': rel_bias}


def reference(x, w_in, b_f, w_out_a, w_out_b, w_o, w_up, w_down, g_mix, g_mlp, g_final, rel_bias):
    B, S = x.shape[0], x.shape[1]
    split_points = [int(p) for p in np.cumsum(SPLIT_SIZES)[:-1]]
    for l in range(DEPTH):
        xn = rmsnorm(x, g_mix[l])
        proj = jnp.einsum('bsd,de->bse', xn, w_in[l])
        qa, ka, va, iq, ik, iw, qb, kb, vb, fl, ga, gb = jnp.split(proj, split_points, axis=-1)
        ya = dsa_attention(qa.reshape(B, S, N_HEADS_A, HEAD_DIM),
                           ka.reshape(B, S, N_HEADS_A, HEAD_DIM),
                           va.reshape(B, S, N_HEADS_A, HEAD_DIM),
                           iq.reshape(B, S, N_IDX_HEADS, IDX_DIM), ik,
                           iw * (N_IDX_HEADS ** -0.5), rel_bias)
        log_f = jax.nn.log_sigmoid(fl.astype(jnp.float32) + b_f[l].astype(jnp.float32))
        yb = fox_attention(qb.reshape(B, S, N_HEADS_B, HEAD_DIM),
                           kb.reshape(B, S, N_HEADS_B, HEAD_DIM),
                           vb.reshape(B, S, N_HEADS_B, HEAD_DIM), log_f)
        pa = jnp.einsum('bse,ed->bsd', ya, w_out_a[l])
        pb = jnp.einsum('bse,ed->bsd', yb, w_out_b[l])
        merged = jax.nn.sigmoid(ga) * pa + jax.nn.sigmoid(gb) * pb
        x = x + jnp.einsum('bsd,de->bse', merged, w_o[l])
        hn = rmsnorm(x, g_mlp[l])
        h = jnp.square(jax.nn.relu(jnp.einsum('bsd,df->bsf', hn, w_up[l])))
        x = x + jnp.einsum('bsf,fd->bsd', h, w_down[l])
    return rmsnorm(x, g_final)
```

```python
import functools
import math

import jax
import jax.numpy as jnp
from jax import lax
from jax.experimental import pallas as pl
from jax.experimental.pallas import tpu as pltpu

F32 = jnp.float32
BF16 = jnp.bfloat16

HEAD_DIM = 128
N_HEADS = 16
D_HEADS = N_HEADS * HEAD_DIM
N_IDX_HEADS = 32
IDX_DIM = 64
CHUNK = 64
TOPK_MAX = 256
N_BUCKETS = 32
MAX_DISTANCE = 1024
EPS = 1e-6

SMALL_W = 128
INT_MIN = -2 ** 31
NEG_INF = float("-inf")

VMEM_LIMIT_BYTES = 56 * 1024 * 1024

ATT_TILE = 256
IDX_KEY_TILE = 128
N_BIAS_OFFSETS = 5


def _cparams(sem):
    return pltpu.CompilerParams(dimension_semantics=sem, vmem_limit_bytes=VMEM_LIMIT_BYTES)


def _tile(n, pref):
    t = min(n, pref)
    assert n % t == 0, (n, pref)
    return t


def _rms_kernel(x_ref, g_ref, o_ref):
    x = x_ref[...]
    y = x * lax.rsqrt(jnp.mean(x * x, axis=-1, keepdims=True) + EPS)
    o_ref[...] = (y * g_ref[...]).astype(o_ref.dtype)


def _rmsnorm(x2d, g, out_dtype):
    m, d = x2d.shape
    tr = _tile(m, 256)
    return pl.pallas_call(
        _rms_kernel,
        grid=(m // tr,),
        in_specs=[pl.BlockSpec((tr, d), lambda i: (i, 0)),
                  pl.BlockSpec((1, d), lambda i: (0, 0))],
        out_specs=pl.BlockSpec((tr, d), lambda i: (i, 0)),
        out_shape=jax.ShapeDtypeStruct((m, d), out_dtype),
        compiler_params=_cparams(("parallel",)),
        name="rmsnorm",
    )(x2d, g.reshape(1, d))


def _rms_small_kernel(x_ref, g_ref, wh_ref, wl_ref, xn_ref, small_ref):
    x = x_ref[...]
    y = x * lax.rsqrt(jnp.mean(x * x, axis=-1, keepdims=True) + EPS) * g_ref[...]
    yh = y.astype(BF16)
    xn_ref[...] = yh
    yl = (y - yh.astype(F32)).astype(BF16)
    wh = wh_ref[...]
    acc = jnp.dot(yh, wh, preferred_element_type=F32)
    acc = acc + jnp.dot(yl, wh, preferred_element_type=F32)
    acc = acc + jnp.dot(yh, wl_ref[...], preferred_element_type=F32)
    small_ref[...] = acc


def _rmsnorm_small(x2d, g, w_small):
    m, d = x2d.shape
    tr = _tile(m, 256)
    wh = w_small.astype(BF16)
    wl = (w_small - wh.astype(F32)).astype(BF16)
    return pl.pallas_call(
        _rms_small_kernel,
        grid=(m // tr,),
        in_specs=[pl.BlockSpec((tr, d), lambda i: (i, 0)),
                  pl.BlockSpec((1, d), lambda i: (0, 0)),
                  pl.BlockSpec((d, SMALL_W), lambda i: (0, 0)),
                  pl.BlockSpec((d, SMALL_W), lambda i: (0, 0))],
        out_specs=[pl.BlockSpec((tr, d), lambda i: (i, 0)),
                   pl.BlockSpec((tr, SMALL_W), lambda i: (i, 0))],
        out_shape=[jax.ShapeDtypeStruct((m, d), BF16),
                   jax.ShapeDtypeStruct((m, SMALL_W), F32)],
        compiler_params=_cparams(("parallel",)),
        name="rmsnorm_small_proj",
    )(x2d, g.reshape(1, d), wh, wl)


def _mm_kernel(*refs, nk, epilogue):
    if epilogue == "residual":
        a_ref, w_ref, r_ref, o_ref = refs[:4]
        scratch = refs[4:]
    else:
        a_ref, w_ref, o_ref = refs[:3]
        scratch = refs[3:]

    def finish(acc):
        if epilogue == "relu2":
            acc = jnp.square(jnp.maximum(acc, 0.0))
        elif epilogue == "residual":
            acc = acc + r_ref[...]
        o_ref[...] = acc.astype(o_ref.dtype)

    part = jnp.dot(a_ref[...], w_ref[...], preferred_element_type=F32)
    if nk == 1:
        finish(part)
        return
    acc_ref = scratch[0]
    k = pl.program_id(2)

    @pl.when(k == 0)
    def _():
        acc_ref[...] = part

    @pl.when(jnp.logical_and(k > 0, k < nk - 1))
    def _():
        acc_ref[...] += part

    @pl.when(k == nk - 1)
    def _():
        finish(acc_ref[...] + part)


def _matmul(a, w, *, out_dtype, epilogue="plain", residual=None, tm=1024, tn=1024, tk=4096,
            name="matmul"):
    m, kdim = a.shape
    n = w.shape[1]
    tm, tn, tk = _tile(m, tm), _tile(n, tn), _tile(kdim, tk)
    nk = kdim // tk
    in_specs = [pl.BlockSpec((tm, tk), lambda i, j, k: (i, k)),
                pl.BlockSpec((tk, tn), lambda i, j, k: (k, j))]
    args = [a, w]
    if epilogue == "residual":
        in_specs.append(pl.BlockSpec((tm, tn), lambda i, j, k: (i, j)))
        args.append(residual)
    scratch = [pltpu.VMEM((tm, tn), F32)] if nk > 1 else []
    return pl.pallas_call(
        functools.partial(_mm_kernel, nk=nk, epilogue=epilogue),
        grid=(m // tm, n // tn, nk),
        in_specs=in_specs,
        out_specs=pl.BlockSpec((tm, tn), lambda i, j, k: (i, j)),
        out_shape=jax.ShapeDtypeStruct((m, n), out_dtype),
        scratch_shapes=scratch,
        compiler_params=_cparams(("parallel", "parallel", "arbitrary")),
        name=name,
    )(*args)


def _merge_kernel(ya_ref, yb_ref, wa_ref, wb_ref, ga_ref, gb_ref, o_ref):
    pa = jnp.dot(ya_ref[...], wa_ref[...], preferred_element_type=F32)
    pb = jnp.dot(yb_ref[...], wb_ref[...], preferred_element_type=F32)
    ga = jax.nn.sigmoid(ga_ref[...].astype(F32))
    gb = jax.nn.sigmoid(gb_ref[...].astype(F32))
    o_ref[...] = (ga * pa + gb * pb).astype(o_ref.dtype)


def _gated_merge(ya, yb, wa, wb, proj, ga_col, gb_col, d_model):
    m, e = ya.shape
    tm, tn = _tile(m, 1024), _tile(d_model, 512)
    assert ga_col % tn == 0 and gb_col % tn == 0
    ga_blk, gb_blk = ga_col // tn, gb_col // tn
    return pl.pallas_call(
        _merge_kernel,
        grid=(m // tm, d_model // tn),
        in_specs=[pl.BlockSpec((tm, e), lambda i, j: (i, 0)),
                  pl.BlockSpec((tm, e), lambda i, j: (i, 0)),
                  pl.BlockSpec((e, tn), lambda i, j: (0, j)),
                  pl.BlockSpec((e, tn), lambda i, j: (0, j)),
                  pl.BlockSpec((tm, tn), lambda i, j: (i, ga_blk + j)),
                  pl.BlockSpec((tm, tn), lambda i, j: (i, gb_blk + j))],
        out_specs=pl.BlockSpec((tm, tn), lambda i, j: (i, j)),
        out_shape=jax.ShapeDtypeStruct((m, d_model), BF16),
        compiler_params=_cparams(("parallel", "parallel")),
        name="gated_merge",
    )(ya, yb, wa, wb, proj, proj)


def _indexer_kernel(ik_ref, qt_ref, w_ref, mask_ref, key_ref, *, top_k, seq):
    tq = ATT_TILE
    qi = pl.program_id(1)
    q0 = qi * tq
    n_rows = q0 + tq

    t_idx = q0 + lax.broadcasted_iota(jnp.int32, (IDX_KEY_TILE, tq), 1)
    limit = (t_idx // CHUNK + 1) * CHUNK
    s_loc = lax.broadcasted_iota(jnp.int32, (IDX_KEY_TILE, tq), 0)

    def score_body(j, carry):
        row0 = pl.multiple_of(j * IDX_KEY_TILE, IDX_KEY_TILE)
        k_tile = ik_ref[0, pl.ds(row0, IDX_KEY_TILE), :]
        acc = jnp.zeros((IDX_KEY_TILE, tq), F32)
        for h in range(N_IDX_HEADS):
            sc = jnp.dot(k_tile, qt_ref[0, 0, :, h * tq:(h + 1) * tq],
                         preferred_element_type=F32)
            acc = acc + jnp.maximum(sc, 0.0) * w_ref[0, 0, :, h * tq:(h + 1) * tq]
        bits = pltpu.bitcast(acc, jnp.int32)
        key = bits ^ ((bits >> 31) & jnp.int32(0x7FFFFFFF))
        key = jnp.where(row0 + s_loc < limit, key, jnp.int32(INT_MIN))
        key_ref[pl.ds(row0, IDX_KEY_TILE), :] = key
        return carry

    lax.fori_loop(0, n_rows // IDX_KEY_TILE, score_body, 0)

    rc = ATT_TILE
    n_blk = n_rows // rc

    def count_ge(cand):
        def body(r, cnt):
            blk = key_ref[pl.ds(pl.multiple_of(r * rc, rc), rc), :]
            ones = jnp.where(blk >= cand, jnp.int32(1), jnp.int32(0))
            return cnt + ones.reshape(rc // 8, 8, tq).sum(axis=0)
        cnt = lax.fori_loop(0, n_blk, body, jnp.zeros((8, tq), jnp.int32))
        return cnt.sum(axis=0, keepdims=True)

    zero = jnp.zeros((1, tq), jnp.int32)
    thr = jnp.where(count_ge(zero) >= top_k, zero, jnp.int32(INT_MIN))

    def bit_body(i, thr):
        cand = thr + jnp.left_shift(jnp.int32(1), jnp.int32(30) - i)
        return jnp.where(count_ge(cand) >= top_k, cand, thr)

    thr = lax.fori_loop(0, 31, bit_body, thr)
    thr = jnp.maximum(thr, jnp.int32(INT_MIN + 1))

    def emit_body(r, carry):
        row0 = pl.multiple_of(r * rc, rc)
        blk = key_ref[pl.ds(row0, rc), :]
        mask_ref[0, pl.ds(row0, rc), :] = jnp.where(blk >= thr, jnp.int32(1), jnp.int32(0)).astype(jnp.int8)
        return carry

    lax.fori_loop(0, n_blk, emit_body, 0)

    def zero_body(r, carry):
        row0 = pl.multiple_of(r * rc, rc)
        mask_ref[0, pl.ds(row0, rc), :] = jnp.zeros((rc, tq), jnp.int8)
        return carry

    lax.fori_loop(n_blk, seq // rc, zero_body, 0)


def _indexer_mask(ik, qt, w_row, top_k):
    b, s, _ = ik.shape
    tq = ATT_TILE
    n_q = s // tq
    return pl.pallas_call(
        functools.partial(_indexer_kernel, top_k=top_k, seq=s),
        grid=(b, n_q),
        in_specs=[pl.BlockSpec((1, s, IDX_DIM), lambda bi, qi: (bi, 0, 0)),
                  pl.BlockSpec((1, 1, IDX_DIM, N_IDX_HEADS * tq), lambda bi, qi: (bi, qi, 0, 0)),
                  pl.BlockSpec((1, 1, 1, N_IDX_HEADS * tq), lambda bi, qi: (bi, qi, 0, 0))],
        out_specs=pl.BlockSpec((1, s, tq), lambda bi, qi: (bi, 0, qi)),
        out_shape=jax.ShapeDtypeStruct((b, s, s), jnp.int8),
        scratch_shapes=[pltpu.VMEM((s, tq), jnp.int32)],
        compiler_params=_cparams(("parallel", "arbitrary")),
        name="dsa_indexer_topk_mask",
    )(ik, qt, w_row)


def _softmax_step(s, m, l, acc, vt_tile):
    m_new = jnp.maximum(m, jnp.max(s, axis=0, keepdims=True))
    m_safe = jnp.where(m_new == NEG_INF, 0.0, m_new)
    p = jnp.exp(s - m_safe)
    alpha = jnp.exp(m - m_safe)
    l = alpha * l + jnp.sum(p, axis=0, keepdims=True)
    acc = acc * alpha + jnp.dot(vt_tile, p.astype(BF16), preferred_element_type=F32)
    return m_new, l, acc


def _attn_init(tq):
    return (jnp.full((1, tq), NEG_INF, F32), jnp.zeros((1, tq), F32),
            jnp.zeros((HEAD_DIM, tq), F32))


def _dsa_attn_kernel(qt_ref, k_ref, vt_ref, mask_ref, bias_ref, o_ref):
    tq = tk = ATT_TILE
    qi = pl.program_id(2)
    qt = qt_ref[0, 0]

    def body(j, carry):
        m, l, acc = carry
        row0 = pl.multiple_of(j * tk, tk)
        s = jnp.dot(k_ref[0, pl.ds(row0, tk), :], qt, preferred_element_type=F32)
        off = jnp.minimum(qi - j, N_BIAS_OFFSETS - 1)
        s = s + bias_ref[0, off]
        sel = mask_ref[0, pl.ds(row0, tk), :].astype(jnp.int32)
        s = jnp.where(sel != 0, s, NEG_INF)
        return _softmax_step(s, m, l, acc, vt_ref[0, 0, j])

    m, l, acc = lax.fori_loop(0, qi + 1, body, _attn_init(tq))
    o_ref[0] = (acc / l).T.astype(o_ref.dtype)


def _dsa_attention(qt, proj3, k_blk0, vt, mask, bias):
    b, h, _, s = qt.shape
    tq = ATT_TILE
    return pl.pallas_call(
        _dsa_attn_kernel,
        grid=(b, h, s // tq),
        in_specs=[pl.BlockSpec((1, 1, HEAD_DIM, tq), lambda bi, hi, qi: (bi, hi, 0, qi)),
                  pl.BlockSpec((1, s, HEAD_DIM), lambda bi, hi, qi: (bi, 0, k_blk0 + hi)),
                  pl.BlockSpec((1, 1, s // tq, HEAD_DIM, tq), lambda bi, hi, qi: (bi, hi, 0, 0, 0)),
                  pl.BlockSpec((1, s, tq), lambda bi, hi, qi: (bi, 0, qi)),
                  pl.BlockSpec((1, N_BIAS_OFFSETS, tq, tq), lambda bi, hi, qi: (hi, 0, 0, 0))],
        out_specs=pl.BlockSpec((1, tq, HEAD_DIM), lambda bi, hi, qi: (bi, qi, hi)),
        out_shape=jax.ShapeDtypeStruct((b, s, h * HEAD_DIM), BF16),
        compiler_params=_cparams(("parallel", "parallel", "arbitrary")),
        name="dsa_masked_attention",
    )(qt, proj3, vt, mask, bias)


def _fox_attn_kernel(qt_ref, k_ref, vt_ref, ccol_ref, crow_ref, o_ref):
    tq = tk = ATT_TILE
    qi = pl.program_id(2)
    qt = qt_ref[0, 0]
    cq = crow_ref[0, 0]

    def logits(j):
        row0 = pl.multiple_of(j * tk, tk)
        s = jnp.dot(k_ref[0, pl.ds(row0, tk), :], qt, preferred_element_type=F32)
        return (s + cq) - ccol_ref[0, 0, pl.ds(row0, tk), :]

    def body(j, carry):
        m, l, acc = carry
        return _softmax_step(logits(j), m, l, acc, vt_ref[0, 0, j])

    m, l, acc = lax.fori_loop(0, qi, body, _attn_init(tq))
    s = logits(qi)
    row = lax.broadcasted_iota(jnp.int32, (tk, tq), 0)
    col = lax.broadcasted_iota(jnp.int32, (tk, tq), 1)
    s = jnp.where(row <= col, s, NEG_INF)
    m, l, acc = _softmax_step(s, m, l, acc, vt_ref[0, 0, qi])
    o_ref[0] = (acc / l).T.astype(o_ref.dtype)


def _fox_attention(qt, proj3, k_blk0, vt, c_col, c_row):
    b, h, _, s = qt.shape
    tq = ATT_TILE
    return pl.pallas_call(
        _fox_attn_kernel,
        grid=(b, h, s // tq),
        in_specs=[pl.BlockSpec((1, 1, HEAD_DIM, tq), lambda bi, hi, qi: (bi, hi, 0, qi)),
                  pl.BlockSpec((1, s, HEAD_DIM), lambda bi, hi, qi: (bi, 0, k_blk0 + hi)),
                  pl.BlockSpec((1, 1, s // tq, HEAD_DIM, tq), lambda bi, hi, qi: (bi, hi, 0, 0, 0)),
                  pl.BlockSpec((1, 1, s, 1), lambda bi, hi, qi: (bi, hi, 0, 0)),
                  pl.BlockSpec((1, 1, 1, tq), lambda bi, hi, qi: (bi, hi, 0, qi))],
        out_specs=pl.BlockSpec((1, tq, HEAD_DIM), lambda bi, hi, qi: (bi, qi, hi)),
        out_shape=jax.ShapeDtypeStruct((b, s, h * HEAD_DIM), BF16),
        compiler_params=_cparams(("parallel", "parallel", "arbitrary")),
        name="fox_attention",
    )(qt, proj3, vt, c_col, c_row)


def _t5_bucket(rel):
    half = N_BUCKETS // 2
    max_exact = half // 2
    base = jnp.where(rel > 0, half, 0)
    n = jnp.abs(rel)
    nf = jnp.maximum(n, max_exact).astype(jnp.float32)
    large = max_exact + (jnp.log(nf / max_exact) / math.log(MAX_DISTANCE / max_exact)
                         * (half - max_exact)).astype(jnp.int32)
    large = jnp.minimum(large, half - 1)
    return base + jnp.where(n < max_exact, n, large)


def _bias_tiles(rel_bias):
    t = ATT_TILE
    half, max_exact = N_BUCKETS // 2, N_BUCKETS // 4
    last_bucket_start = max_exact * (MAX_DISTANCE / max_exact) ** ((half - 1 - max_exact) / (half - max_exact))
    assert (N_BIAS_OFFSETS - 1) * t - (t - 1) > last_bucket_start + 1
    i = jnp.arange(t, dtype=jnp.int32)[:, None]
    j = jnp.arange(t, dtype=jnp.int32)[None, :]
    rel = jnp.stack([i - j - o * t for o in range(N_BIAS_OFFSETS)])
    bias = rel_bias[_t5_bucket(rel)]
    return jnp.transpose(bias, (3, 0, 1, 2)).astype(F32)


def _heads_t(a):
    b, s, _ = a.shape
    return jnp.transpose(a.reshape(b, s, N_HEADS, HEAD_DIM), (0, 2, 3, 1))


def _heads_t_tiled(a):
    b, s, _ = a.shape
    t = ATT_TILE
    return jnp.transpose(a.reshape(b, s // t, t, N_HEADS, HEAD_DIM), (0, 3, 1, 4, 2))


def kernel(x, w_in, b_f, w_out_a, w_out_b, w_o, w_up, w_down, g_mix, g_mlp, g_final, rel_bias):
    b, s, d = x.shape
    m = b * s
    depth = w_in.shape[0]
    assert s % ATT_TILE == 0 and ATT_TILE % CHUNK == 0
    top_k = min(TOPK_MAX, s // 4)

    sizes = (D_HEADS, D_HEADS, D_HEADS, N_IDX_HEADS * IDX_DIM, IDX_DIM, N_IDX_HEADS,
             D_HEADS, D_HEADS, D_HEADS, N_HEADS, d, d)
    offs = [0]
    for sz in sizes:
        offs.append(offs[-1] + sz)
    (o_qa, o_ka, o_va, o_iq, o_ik, o_iw, o_qb, o_kb, o_vb, o_fl, o_ga, o_gb, _) = offs

    p_qa, p_ka, p_va = 0, D_HEADS, 2 * D_HEADS
    p_iq = 3 * D_HEADS
    p_qb = p_iq + N_IDX_HEADS * IDX_DIM
    p_kb, p_vb = p_qb + D_HEADS, p_qb + 2 * D_HEADS
    p_ga = p_qb + 3 * D_HEADS
    p_gb = p_ga + d
    n_proj = p_gb + d

    att_scale = HEAD_DIM ** -0.5
    bias = _bias_tiles(rel_bias)

    x2 = x.reshape(m, d)
    for l in range(depth):
        w = w_in[l]

        def cols(o, n):
            return lax.slice_in_dim(w, o, o + n, axis=1)

        w_main = jnp.concatenate([
            cols(o_qa, D_HEADS) * att_scale, cols(o_ka, 2 * D_HEADS),
            cols(o_iq, N_IDX_HEADS * IDX_DIM),
            cols(o_qb, D_HEADS) * att_scale, cols(o_kb, 2 * D_HEADS),
            cols(o_ga, 2 * d)], axis=1).astype(BF16)
        w_small = jnp.concatenate([
            cols(o_ik, IDX_DIM) * (IDX_DIM ** -0.5),
            cols(o_iw, N_IDX_HEADS) * (N_IDX_HEADS ** -0.5),
            cols(o_fl, N_HEADS),
            jnp.zeros((d, SMALL_W - IDX_DIM - N_IDX_HEADS - N_HEADS), F32)], axis=1)

        xn, small = _rmsnorm_small(x2, g_mix[l], w_small)
        proj = _matmul(xn, w_main, out_dtype=BF16, name="in_proj")
        proj3 = proj.reshape(b, s, n_proj)

        tq = ATT_TILE
        n_q = s // tq
        ik = small[:, :IDX_DIM].astype(BF16).reshape(b, s, IDX_DIM)
        iq = lax.slice_in_dim(proj3, p_iq, p_iq + N_IDX_HEADS * IDX_DIM, axis=2)
        iq_t = jnp.transpose(iq.reshape(b, n_q, tq, N_IDX_HEADS, IDX_DIM), (0, 1, 4, 3, 2))
        iq_t = iq_t.reshape(b, n_q, IDX_DIM, N_IDX_HEADS * tq)
        iw = small[:, IDX_DIM:IDX_DIM + N_IDX_HEADS].reshape(b, n_q, tq, N_IDX_HEADS)
        iw_row = jnp.transpose(iw, (0, 1, 3, 2)).reshape(b, n_q, 1, N_IDX_HEADS * tq)
        mask = _indexer_mask(ik, iq_t, iw_row, top_k)

        qa_t = _heads_t(lax.slice_in_dim(proj3, p_qa, p_qa + D_HEADS, axis=2))
        va_t = _heads_t_tiled(lax.slice_in_dim(proj3, p_va, p_va + D_HEADS, axis=2))
        ya = _dsa_attention(qa_t, proj3, p_ka // HEAD_DIM, va_t, mask, bias)

        fl = small[:, IDX_DIM + N_IDX_HEADS:IDX_DIM + N_IDX_HEADS + N_HEADS].reshape(b, s, N_HEADS)
        log_f = jax.nn.log_sigmoid(fl + b_f[l].astype(F32))
        cum = jnp.transpose(jnp.cumsum(log_f, axis=1), (0, 2, 1))
        qb_t = _heads_t(lax.slice_in_dim(proj3, p_qb, p_qb + D_HEADS, axis=2))
        vb_t = _heads_t_tiled(lax.slice_in_dim(proj3, p_vb, p_vb + D_HEADS, axis=2))
        yb = _fox_attention(qb_t, proj3, p_kb // HEAD_DIM, vb_t,
                            cum[:, :, :, None], cum[:, :, None, :])

        merged = _gated_merge(ya.reshape(m, D_HEADS), yb.reshape(m, D_HEADS),
                              w_out_a[l].astype(BF16), w_out_b[l].astype(BF16),
                              proj, p_ga, p_gb, d)
        x2 = _matmul(merged, w_o[l].astype(BF16), out_dtype=F32, epilogue="residual",
                     residual=x2, tn=512, name="out_proj_residual")

        hn = _rmsnorm(x2, g_mlp[l], BF16)
        hid = _matmul(hn, w_up[l].astype(BF16), out_dtype=BF16, epilogue="relu2", name="mlp_up_relu2")
        x2 = _matmul(hid, w_down[l].astype(BF16), out_dtype=F32, epilogue="residual",
                     residual=x2, tk=2048, name="mlp_down_residual")

    return _rmsnorm(x2, g_final, F32).reshape(b, s, d)
```

```python
import functools
import math

import jax
import jax.numpy as jnp
from jax import lax
from jax.experimental import pallas as pl
from jax.experimental.pallas import tpu as pltpu

F32 = jnp.float32
BF16 = jnp.bfloat16

HEAD_DIM = 128
N_HEADS = 16
D_HEADS = N_HEADS * HEAD_DIM
N_IDX_HEADS = 32
IDX_DIM = 64
CHUNK = 64
TOPK_MAX = 256
N_BUCKETS = 32
MAX_DISTANCE = 1024
EPS = 1e-6

SMALL_W = 128
INT_MIN = -2 ** 31
NEG_INF = float("-inf")

VMEM_LIMIT_BYTES = 56 * 1024 * 1024

ATT_TILE = 512
ATT_HEADS = 4
ATT_KDIM = 256
IDX_Q_TILE = 256
IDX_KEY_TILE = 128
BIAS_TILE = 256
N_BIAS_OFFSETS = 6
LOG2E = math.log2(math.e)


def _cparams(sem):
    return pltpu.CompilerParams(dimension_semantics=sem, vmem_limit_bytes=VMEM_LIMIT_BYTES)


def _tile(n, pref):
    t = min(n, pref)
    assert n % t == 0, (n, pref)
    return t


def _rms_kernel(x_ref, g_ref, o_ref):
    x = x_ref[...]
    y = x * lax.rsqrt(jnp.mean(x * x, axis=-1, keepdims=True) + EPS)
    o_ref[...] = (y * g_ref[...]).astype(o_ref.dtype)


def _rmsnorm(x2d, g, out_dtype):
    m, d = x2d.shape
    tr = _tile(m, 256)
    return pl.pallas_call(
        _rms_kernel,
        grid=(m // tr,),
        in_specs=[pl.BlockSpec((tr, d), lambda i: (i, 0)),
                  pl.BlockSpec((1, d), lambda i: (0, 0))],
        out_specs=pl.BlockSpec((tr, d), lambda i: (i, 0)),
        out_shape=jax.ShapeDtypeStruct((m, d), out_dtype),
        compiler_params=_cparams(("parallel",)),
        name="rmsnorm",
    )(x2d, g.reshape(1, d))


def _rms_small_kernel(x_ref, g_ref, wh_ref, wl_ref, xn_ref, small_ref):
    x = x_ref[...]
    y = x * lax.rsqrt(jnp.mean(x * x, axis=-1, keepdims=True) + EPS) * g_ref[...]
    yh = y.astype(BF16)
    xn_ref[...] = yh
    yl = (y - yh.astype(F32)).astype(BF16)
    wh = wh_ref[...]
    acc = jnp.dot(yh, wh, preferred_element_type=F32)
    acc = acc + jnp.dot(yl, wh, preferred_element_type=F32)
    acc = acc + jnp.dot(yh, wl_ref[...], preferred_element_type=F32)
    small_ref[...] = acc


def _rmsnorm_small(x2d, g, w_small):
    m, d = x2d.shape
    tr = _tile(m, 256)
    wh = w_small.astype(BF16)
    wl = (w_small - wh.astype(F32)).astype(BF16)
    return pl.pallas_call(
        _rms_small_kernel,
        grid=(m // tr,),
        in_specs=[pl.BlockSpec((tr, d), lambda i: (i, 0)),
                  pl.BlockSpec((1, d), lambda i: (0, 0)),
                  pl.BlockSpec((d, SMALL_W), lambda i: (0, 0)),
                  pl.BlockSpec((d, SMALL_W), lambda i: (0, 0))],
        out_specs=[pl.BlockSpec((tr, d), lambda i: (i, 0)),
                   pl.BlockSpec((tr, SMALL_W), lambda i: (i, 0))],
        out_shape=[jax.ShapeDtypeStruct((m, d), BF16),
                   jax.ShapeDtypeStruct((m, SMALL_W), F32)],
        compiler_params=_cparams(("parallel",)),
        name="rmsnorm_small_proj",
    )(x2d, g.reshape(1, d), wh, wl)


def _mm_kernel(*refs, nk, epilogue):
    if epilogue == "residual":
        a_ref, w_ref, r_ref, o_ref = refs[:4]
        scratch = refs[4:]
    else:
        a_ref, w_ref, o_ref = refs[:3]
        scratch = refs[3:]

    def finish(acc):
        if epilogue == "relu2":
            acc = jnp.square(jnp.maximum(acc, 0.0))
        elif epilogue == "residual":
            acc = acc + r_ref[...]
        o_ref[...] = acc.astype(o_ref.dtype)

    part = jnp.dot(a_ref[...], w_ref[...], preferred_element_type=F32)
    if nk == 1:
        finish(part)
        return
    acc_ref = scratch[0]
    k = pl.program_id(2)

    @pl.when(k == 0)
    def _():
        acc_ref[...] = part

    @pl.when(jnp.logical_and(k > 0, k < nk - 1))
    def _():
        acc_ref[...] += part

    @pl.when(k == nk - 1)
    def _():
        finish(acc_ref[...] + part)


def _matmul(a, w, *, out_dtype, epilogue="plain", residual=None, tm=1024, tn=1024, tk=4096,
            name="matmul"):
    m, kdim = a.shape
    n = w.shape[1]
    tm, tn, tk = _tile(m, tm), _tile(n, tn), _tile(kdim, tk)
    nk = kdim // tk
    in_specs = [pl.BlockSpec((tm, tk), lambda i, j, k: (i, k)),
                pl.BlockSpec((tk, tn), lambda i, j, k: (k, j))]
    args = [a, w]
    if epilogue == "residual":
        in_specs.append(pl.BlockSpec((tm, tn), lambda i, j, k: (i, j)))
        args.append(residual)
    scratch = [pltpu.VMEM((tm, tn), F32)] if nk > 1 else []
    return pl.pallas_call(
        functools.partial(_mm_kernel, nk=nk, epilogue=epilogue),
        grid=(m // tm, n // tn, nk),
        in_specs=in_specs,
        out_specs=pl.BlockSpec((tm, tn), lambda i, j, k: (i, j)),
        out_shape=jax.ShapeDtypeStruct((m, n), out_dtype),
        scratch_shapes=scratch,
        compiler_params=_cparams(("parallel", "parallel", "arbitrary")),
        name=name,
    )(*args)


def _merge_kernel(ya_ref, yb_ref, wa_ref, wb_ref, ga_ref, gb_ref, o_ref):
    pa = jnp.dot(ya_ref[...], wa_ref[...], preferred_element_type=F32)
    pb = jnp.dot(yb_ref[...], wb_ref[...], preferred_element_type=F32)
    ga = jax.nn.sigmoid(ga_ref[...].astype(F32))
    gb = jax.nn.sigmoid(gb_ref[...].astype(F32))
    o_ref[...] = (ga * pa + gb * pb).astype(o_ref.dtype)


def _gated_merge(ya, yb, wa, wb, proj, ga_col, gb_col, d_model):
    m, e = ya.shape
    tm, tn = _tile(m, 1024), _tile(d_model, 512)
    assert ga_col % tn == 0 and gb_col % tn == 0
    ga_blk, gb_blk = ga_col // tn, gb_col // tn
    return pl.pallas_call(
        _merge_kernel,
        grid=(m // tm, d_model // tn),
        in_specs=[pl.BlockSpec((tm, e), lambda i, j: (i, 0)),
                  pl.BlockSpec((tm, e), lambda i, j: (i, 0)),
                  pl.BlockSpec((e, tn), lambda i, j: (0, j)),
                  pl.BlockSpec((e, tn), lambda i, j: (0, j)),
                  pl.BlockSpec((tm, tn), lambda i, j: (i, ga_blk + j)),
                  pl.BlockSpec((tm, tn), lambda i, j: (i, gb_blk + j))],
        out_specs=pl.BlockSpec((tm, tn), lambda i, j: (i, j)),
        out_shape=jax.ShapeDtypeStruct((m, d_model), BF16),
        compiler_params=_cparams(("parallel", "parallel")),
        name="gated_merge",
    )(ya, yb, wa, wb, proj, proj)


def _indexer_kernel(ik_ref, qt_ref, w_ref, mask_ref, key_ref, *, top_k, seq):
    tq = IDX_Q_TILE
    qi = pl.program_id(1)
    q0 = qi * tq
    n_rows = q0 + tq

    t_idx = q0 + lax.broadcasted_iota(jnp.int32, (IDX_KEY_TILE, tq), 1)
    limit = (t_idx // CHUNK + 1) * CHUNK
    s_loc = lax.broadcasted_iota(jnp.int32, (IDX_KEY_TILE, tq), 0)

    def score_body(j, carry):
        row0 = pl.multiple_of(j * IDX_KEY_TILE, IDX_KEY_TILE)
        k_tile = ik_ref[0, pl.ds(row0, IDX_KEY_TILE), :]
        acc = jnp.zeros((IDX_KEY_TILE, tq), F32)
        for h in range(N_IDX_HEADS):
            sc = jnp.dot(k_tile, qt_ref[0, 0, :, h * tq:(h + 1) * tq],
                         preferred_element_type=F32)
            acc = acc + jnp.maximum(sc, 0.0) * w_ref[0, 0, :, h * tq:(h + 1) * tq]
        bits = pltpu.bitcast(acc, jnp.int32)
        key = bits ^ ((bits >> 31) & jnp.int32(0x7FFFFFFF))
        key = jnp.where(row0 + s_loc < limit, key, jnp.int32(INT_MIN))
        key_ref[pl.ds(row0, IDX_KEY_TILE), :] = key
        return carry

    lax.fori_loop(0, n_rows // IDX_KEY_TILE, score_body, 0)

    rc = IDX_Q_TILE
    n_blk = n_rows // rc

    def count_ge(cand):
        def body(r, cnt):
            blk = key_ref[pl.ds(pl.multiple_of(r * rc, rc), rc), :]
            ones = jnp.where(blk >= cand, jnp.int32(1), jnp.int32(0))
            return cnt + ones.reshape(rc // 8, 8, tq).sum(axis=0)
        cnt = lax.fori_loop(0, n_blk, body, jnp.zeros((8, tq), jnp.int32))
        return cnt.sum(axis=0, keepdims=True)

    zero = jnp.zeros((1, tq), jnp.int32)
    thr = jnp.where(count_ge(zero) >= top_k, zero, jnp.int32(INT_MIN))

    def bit_body(i, thr):
        cand = thr + jnp.left_shift(jnp.int32(1), jnp.int32(30) - i)
        return jnp.where(count_ge(cand) >= top_k, cand, thr)

    thr = lax.fori_loop(0, 31, bit_body, thr)
    thr = jnp.maximum(thr, jnp.int32(INT_MIN + 1))

    def emit_body(r, carry):
        row0 = pl.multiple_of(r * rc, rc)
        blk = key_ref[pl.ds(row0, rc), :]
        mask_ref[0, pl.ds(row0, rc), :] = jnp.where(blk >= thr, jnp.int32(1), jnp.int32(0)).astype(jnp.int8)
        return carry

    lax.fori_loop(0, n_blk, emit_body, 0)

    def zero_body(r, carry):
        row0 = pl.multiple_of(r * rc, rc)
        mask_ref[0, pl.ds(row0, rc), :] = jnp.zeros((rc, tq), jnp.int8)
        return carry

    lax.fori_loop(n_blk, seq // rc, zero_body, 0)


def _indexer_mask(ik, qt, w_row, top_k):
    b, s, _ = ik.shape
    tq = IDX_Q_TILE
    n_q = s // tq
    return pl.pallas_call(
        functools.partial(_indexer_kernel, top_k=top_k, seq=s),
        grid=(b, n_q),
        in_specs=[pl.BlockSpec((1, s, IDX_DIM), lambda bi, qi: (bi, 0, 0)),
                  pl.BlockSpec((1, 1, IDX_DIM, N_IDX_HEADS * tq), lambda bi, qi: (bi, qi, 0, 0)),
                  pl.BlockSpec((1, 1, 1, N_IDX_HEADS * tq), lambda bi, qi: (bi, qi, 0, 0))],
        out_specs=pl.BlockSpec((1, s, tq), lambda bi, qi: (bi, 0, qi)),
        out_shape=jax.ShapeDtypeStruct((b, s, s), jnp.int8),
        scratch_shapes=[pltpu.VMEM((s, tq), jnp.int32)],
        compiler_params=_cparams(("parallel", "arbitrary")),
        name="dsa_indexer_topk_mask",
    )(ik, qt, w_row)


def _chain_update(g, s, m, l, acc_ref, vt_tile, *, guard):
    m_new = jnp.maximum(m, jnp.max(s, axis=0, keepdims=True))
    m_ref = jnp.where(m_new == NEG_INF, 0.0, m_new) if guard else m_new
    p = jnp.exp2(s - m_ref)
    alpha = jnp.exp2(m - m_ref)
    l_new = alpha * l + jnp.sum(p, axis=0, keepdims=True)
    acc_ref[g] = acc_ref[g] * alpha + jnp.dot(vt_tile, p.astype(BF16), preferred_element_type=F32)
    return m_new, l_new


def _chains_init(acc_ref, tq):
    acc_ref[...] = jnp.zeros(acc_ref.shape, F32)
    one = (jnp.full((1, tq), NEG_INF, F32), jnp.zeros((1, tq), F32))
    return tuple(one for _ in range(ATT_HEADS))


def _chains_finish(state, acc_ref, o_ref):
    for g, (_, l) in enumerate(state):
        o_ref[0, :, g * HEAD_DIM:(g + 1) * HEAD_DIM] = (acc_ref[g] / l).T.astype(o_ref.dtype)


def _dsa_attn_kernel(qt_ref, k_ref, vt_ref, mask_ref, bias_ref, o_ref, acc_ref):
    tq = tk = ATT_TILE
    nb = tk // BIAS_TILE
    qi = pl.program_id(2)

    def body(j, state):
        row0 = pl.multiple_of(j * tk, tk)
        sel = mask_ref[0, pl.ds(row0, tk), :].astype(jnp.int32)
        hide = jnp.where(sel != 0, 0.0, NEG_INF)
        new_state = []
        for g in range(ATT_HEADS):
            s = jnp.dot(k_ref[0, pl.ds(row0, tk), g * HEAD_DIM:(g + 1) * HEAD_DIM], qt_ref[0, g],
                        preferred_element_type=F32)
            rows = []
            for a in range(nb):
                rows.append(jnp.concatenate(
                    [bias_ref[g, jnp.minimum(1 + nb * (qi - j) - a + bb, N_BIAS_OFFSETS - 1)]
                     for bb in range(nb)], axis=1))
            s = (s + jnp.concatenate(rows, axis=0)) + hide
            m, l = state[g]
            new_state.append(_chain_update(g, s, m, l, acc_ref, vt_ref[0, g, j], guard=True))
        return tuple(new_state)

    state = lax.fori_loop(0, qi + 1, body, _chains_init(acc_ref, tq))
    _chains_finish(state, acc_ref, o_ref)


def _dsa_attention(qt, proj3, k_blk0, vt, mask, bias):
    b, h, _, s = qt.shape
    tq, g = ATT_TILE, ATT_HEADS
    assert k_blk0 % g == 0 and h % g == 0
    once = pl.Buffered(1)
    return pl.pallas_call(
        _dsa_attn_kernel,
        grid=(b, h // g, s // tq),
        in_specs=[pl.BlockSpec((1, g, HEAD_DIM, tq), lambda bi, hi, qi: (bi, hi, 0, qi)),
                  pl.BlockSpec((1, s, g * HEAD_DIM), lambda bi, hi, qi: (bi, 0, k_blk0 // g + hi),
                               pipeline_mode=once),
                  pl.BlockSpec((1, g, s // tq, HEAD_DIM, tq), lambda bi, hi, qi: (bi, hi, 0, 0, 0),
                               pipeline_mode=once),
                  pl.BlockSpec((1, s, tq), lambda bi, hi, qi: (bi, 0, qi)),
                  pl.BlockSpec((g, N_BIAS_OFFSETS, BIAS_TILE, BIAS_TILE),
                               lambda bi, hi, qi: (hi, 0, 0, 0), pipeline_mode=once)],
        out_specs=pl.BlockSpec((1, tq, g * HEAD_DIM), lambda bi, hi, qi: (bi, qi, hi)),
        out_shape=jax.ShapeDtypeStruct((b, s, h * HEAD_DIM), BF16),
        scratch_shapes=[pltpu.VMEM((g, HEAD_DIM, tq), F32)],
        compiler_params=_cparams(("parallel", "parallel", "arbitrary")),
        name="dsa_masked_attention",
    )(qt, proj3, vt, mask, bias)


def _fox_attn_kernel(qt_ref, k_ref, vt_ref, o_ref, acc_ref):
    tq = tk = ATT_TILE
    qi = pl.program_id(2)

    def logits(g, j):
        row0 = pl.multiple_of(j * tk, tk)
        return jnp.dot(k_ref[0, g, pl.ds(row0, tk), :], qt_ref[0, g], preferred_element_type=F32)

    def body(j, state):
        return tuple(_chain_update(g, logits(g, j), m, l, acc_ref, vt_ref[0, g, j], guard=False)
                     for g, (m, l) in enumerate(state))

    state = lax.fori_loop(0, qi, body, _chains_init(acc_ref, tq))
    row = lax.broadcasted_iota(jnp.int32, (tk, tq), 0)
    col = lax.broadcasted_iota(jnp.int32, (tk, tq), 1)
    visible = row <= col
    state = tuple(_chain_update(g, jnp.where(visible, logits(g, qi), NEG_INF), m, l, acc_ref,
                                vt_ref[0, g, qi], guard=False)
                  for g, (m, l) in enumerate(state))
    _chains_finish(state, acc_ref, o_ref)


def _fox_attention(qt, k_aug, vt):
    b, h, kd, s = qt.shape
    tq, g = ATT_TILE, ATT_HEADS
    once = pl.Buffered(1)
    return pl.pallas_call(
        _fox_attn_kernel,
        grid=(b, h // g, s // tq),
        in_specs=[pl.BlockSpec((1, g, kd, tq), lambda bi, hi, qi: (bi, hi, 0, qi)),
                  pl.BlockSpec((1, g, s, kd), lambda bi, hi, qi: (bi, hi, 0, 0), pipeline_mode=once),
                  pl.BlockSpec((1, g, s // tq, HEAD_DIM, tq), lambda bi, hi, qi: (bi, hi, 0, 0, 0),
                               pipeline_mode=once)],
        out_specs=pl.BlockSpec((1, tq, g * HEAD_DIM), lambda bi, hi, qi: (bi, qi, hi)),
        out_shape=jax.ShapeDtypeStruct((b, s, h * HEAD_DIM), BF16),
        scratch_shapes=[pltpu.VMEM((g, HEAD_DIM, tq), F32)],
        compiler_params=_cparams(("parallel", "parallel", "arbitrary")),
        name="fox_attention",
    )(qt, k_aug, vt)


def _t5_bucket(rel):
    half = N_BUCKETS // 2
    max_exact = half // 2
    base = jnp.where(rel > 0, half, 0)
    n = jnp.abs(rel)
    nf = jnp.maximum(n, max_exact).astype(jnp.float32)
    large = max_exact + (jnp.log(nf / max_exact) / math.log(MAX_DISTANCE / max_exact)
                         * (half - max_exact)).astype(jnp.int32)
    large = jnp.minimum(large, half - 1)
    return base + jnp.where(n < max_exact, n, large)


def _bias_tiles(rel_bias):
    t = BIAS_TILE
    half, max_exact = N_BUCKETS // 2, N_BUCKETS // 4
    last_bucket_start = max_exact * (MAX_DISTANCE / max_exact) ** ((half - 1 - max_exact) / (half - max_exact))
    assert (N_BIAS_OFFSETS - 2) * t - (t - 1) > last_bucket_start + 1
    i = jnp.arange(t, dtype=jnp.int32)[:, None]
    j = jnp.arange(t, dtype=jnp.int32)[None, :]
    rel = jnp.stack([i - j + (1 - o) * t for o in range(N_BIAS_OFFSETS)])
    onehot = jax.nn.one_hot(_t5_bucket(rel), N_BUCKETS, dtype=F32)
    return jnp.einsum("oijk,kh->hoij", onehot, rel_bias.astype(F32) * LOG2E,
                      precision=lax.Precision.HIGHEST)


def _heads_t(a):
    b, s, _ = a.shape
    return jnp.transpose(a.reshape(b, s, N_HEADS, HEAD_DIM), (0, 2, 3, 1))


def _heads_t_tiled(a):
    b, s, _ = a.shape
    t = ATT_TILE
    return jnp.transpose(a.reshape(b, s // t, t, N_HEADS, HEAD_DIM), (0, 3, 1, 4, 2))


def _split3(c):
    hi = c.astype(BF16)
    r = c - hi.astype(F32)
    mid = r.astype(BF16)
    lo = (r - mid.astype(F32)).astype(BF16)
    return hi, mid, lo


def kernel(x, w_in, b_f, w_out_a, w_out_b, w_o, w_up, w_down, g_mix, g_mlp, g_final, rel_bias):
    b, s, d = x.shape
    m = b * s
    depth = w_in.shape[0]
    assert s % ATT_TILE == 0 and ATT_TILE % BIAS_TILE == 0 and IDX_Q_TILE % CHUNK == 0
    top_k = min(TOPK_MAX, s // 4)

    sizes = (D_HEADS, D_HEADS, D_HEADS, N_IDX_HEADS * IDX_DIM, IDX_DIM, N_IDX_HEADS,
             D_HEADS, D_HEADS, D_HEADS, N_HEADS, d, d)
    offs = [0]
    for sz in sizes:
        offs.append(offs[-1] + sz)
    (o_qa, o_ka, o_va, o_iq, o_ik, o_iw, o_qb, o_kb, o_vb, o_fl, o_ga, o_gb, _) = offs

    p_qa, p_ka, p_va = 0, D_HEADS, 2 * D_HEADS
    p_iq = 3 * D_HEADS
    p_qb = p_iq + N_IDX_HEADS * IDX_DIM
    p_kb, p_vb = p_qb + D_HEADS, p_qb + 2 * D_HEADS
    p_ga = p_qb + 3 * D_HEADS
    p_gb = p_ga + d
    n_proj = p_gb + d

    q_scale = HEAD_DIM ** -0.5 * LOG2E
    bias = _bias_tiles(rel_bias)

    x2 = x.reshape(m, d)
    for l in range(depth):
        w = w_in[l]

        def cols(o, n):
            return lax.slice_in_dim(w, o, o + n, axis=1)

        w_main = jnp.concatenate([
            cols(o_qa, D_HEADS) * q_scale, cols(o_ka, 2 * D_HEADS),
            cols(o_iq, N_IDX_HEADS * IDX_DIM),
            cols(o_qb, D_HEADS) * q_scale, cols(o_kb, 2 * D_HEADS),
            cols(o_ga, 2 * d)], axis=1).astype(BF16)
        w_small = jnp.concatenate([
            cols(o_ik, IDX_DIM) * (IDX_DIM ** -0.5),
            cols(o_iw, N_IDX_HEADS) * (N_IDX_HEADS ** -0.5),
            cols(o_fl, N_HEADS),
            jnp.zeros((d, SMALL_W - IDX_DIM - N_IDX_HEADS - N_HEADS), F32)], axis=1)

        xn, small = _rmsnorm_small(x2, g_mix[l], w_small)
        proj = _matmul(xn, w_main, out_dtype=BF16, name="in_proj")
        proj3 = proj.reshape(b, s, n_proj)

        tq = IDX_Q_TILE
        n_q = s // tq
        ik = small[:, :IDX_DIM].astype(BF16).reshape(b, s, IDX_DIM)
        iq = lax.slice_in_dim(proj3, p_iq, p_iq + N_IDX_HEADS * IDX_DIM, axis=2)
        iq_t = jnp.transpose(iq.reshape(b, n_q, tq, N_IDX_HEADS, IDX_DIM), (0, 1, 4, 3, 2))
        iq_t = iq_t.reshape(b, n_q, IDX_DIM, N_IDX_HEADS * tq)
        iw = small[:, IDX_DIM:IDX_DIM + N_IDX_HEADS].reshape(b, n_q, tq, N_IDX_HEADS)
        iw_row = jnp.transpose(iw, (0, 1, 3, 2)).reshape(b, n_q, 1, N_IDX_HEADS * tq)
        mask = _indexer_mask(ik, iq_t, iw_row, top_k)

        qa_t = _heads_t(lax.slice_in_dim(proj3, p_qa, p_qa + D_HEADS, axis=2))
        va_t = _heads_t_tiled(lax.slice_in_dim(proj3, p_va, p_va + D_HEADS, axis=2))
        ya = _dsa_attention(qa_t, proj3, p_ka // HEAD_DIM, va_t, mask, bias)

        fl = small[:, IDX_DIM + N_IDX_HEADS:IDX_DIM + N_IDX_HEADS + N_HEADS].reshape(b, s, N_HEADS)
        log_f = jax.nn.log_sigmoid(fl + b_f[l].astype(F32))
        cum = jnp.transpose(jnp.cumsum(log_f, axis=1), (0, 2, 1)) * LOG2E
        c_hi, c_mid, c_lo = _split3(cum)
        ones = jnp.ones((b, N_HEADS, s), BF16)
        pad = ATT_KDIM - HEAD_DIM - 6
        kb = jnp.transpose(lax.slice_in_dim(proj3, p_kb, p_kb + D_HEADS, axis=2)
                           .reshape(b, s, N_HEADS, HEAD_DIM), (0, 2, 1, 3))
        k_aug = jnp.concatenate(
            [kb, jnp.stack([c_hi, c_mid, c_lo, ones, ones, ones], axis=-1),
             jnp.zeros((b, N_HEADS, s, pad), BF16)], axis=-1)
        qb_t = _heads_t(lax.slice_in_dim(proj3, p_qb, p_qb + D_HEADS, axis=2))
        qb_aug = jnp.concatenate(
            [qb_t, jnp.stack([-ones, -ones, -ones, c_hi, c_mid, c_lo], axis=2),
             jnp.zeros((b, N_HEADS, pad, s), BF16)], axis=2)
        vb_t = _heads_t_tiled(lax.slice_in_dim(proj3, p_vb, p_vb + D_HEADS, axis=2))
        yb = _fox_attention(qb_aug, k_aug, vb_t)

        merged = _gated_merge(ya.reshape(m, D_HEADS), yb.reshape(m, D_HEADS),
                              w_out_a[l].astype(BF16), w_out_b[l].astype(BF16),
                              proj, p_ga, p_gb, d)
        x2 = _matmul(merged, w_o[l].astype(BF16), out_dtype=F32, epilogue="residual",
                     residual=x2, tn=512, name="out_proj_residual")

        hn = _rmsnorm(x2, g_mlp[l], BF16)
        hid = _matmul(hn, w_up[l].astype(BF16), out_dtype=BF16, epilogue="relu2", name="mlp_up_relu2")
        x2 = _matmul(hid, w_down[l].astype(BF16), out_dtype=F32, epilogue="residual",
                     residual=x2, tk=2048, name="mlp_down_residual")

    return _rmsnorm(x2, g_final, F32).reshape(b, s, d)
```

```python
import functools
import math

import jax
import jax.numpy as jnp
from jax import lax
from jax.experimental import pallas as pl
from jax.experimental.pallas import tpu as pltpu

F32 = jnp.float32
BF16 = jnp.bfloat16

HEAD_DIM = 128
N_HEADS = 16
D_HEADS = N_HEADS * HEAD_DIM
N_IDX_HEADS = 32
IDX_DIM = 64
CHUNK = 64
TOPK_MAX = 256
N_BUCKETS = 32
MAX_DISTANCE = 1024
EPS = 1e-6

SMALL_W = 128
INT_MIN = -2 ** 31
NEG_INF = float("-inf")

VMEM_LIMIT_BYTES = 56 * 1024 * 1024

ATT_TILE = 512
ATT_HEADS = 4
ATT_KDIM = 256
ATT_VROWS = HEAD_DIM + 16
IDX_Q_TILE = 256
IDX_KEY_TILE = 128
SEARCH_STEPS_PER_CHECK = 3
BIAS_TILE = 256
N_BIAS_OFFSETS = 6
LOG2E = math.log2(math.e)


def _cparams(sem):
    return pltpu.CompilerParams(dimension_semantics=sem, vmem_limit_bytes=VMEM_LIMIT_BYTES)


def _tile(n, pref):
    t = min(n, pref)
    assert n % t == 0, (n, pref)
    return t


def _rms_kernel(x_ref, g_ref, o_ref):
    x = x_ref[...]
    y = x * lax.rsqrt(jnp.mean(x * x, axis=-1, keepdims=True) + EPS)
    o_ref[...] = (y * g_ref[...]).astype(o_ref.dtype)


def _rmsnorm(x2d, g, out_dtype):
    m, d = x2d.shape
    tr = _tile(m, 256)
    return pl.pallas_call(
        _rms_kernel,
        grid=(m // tr,),
        in_specs=[pl.BlockSpec((tr, d), lambda i: (i, 0)),
                  pl.BlockSpec((1, d), lambda i: (0, 0))],
        out_specs=pl.BlockSpec((tr, d), lambda i: (i, 0)),
        out_shape=jax.ShapeDtypeStruct((m, d), out_dtype),
        compiler_params=_cparams(("parallel",)),
        name="rmsnorm",
    )(x2d, g.reshape(1, d))


def _rms_small_kernel(x_ref, g_ref, wh_ref, wl_ref, xn_ref, small_ref):
    x = x_ref[...]
    y = x * lax.rsqrt(jnp.mean(x * x, axis=-1, keepdims=True) + EPS) * g_ref[...]
    yh = y.astype(BF16)
    xn_ref[...] = yh
    yl = (y - yh.astype(F32)).astype(BF16)
    wh = wh_ref[...]
    acc = jnp.dot(yh, wh, preferred_element_type=F32)
    acc = acc + jnp.dot(yl, wh, preferred_element_type=F32)
    acc = acc + jnp.dot(yh, wl_ref[...], preferred_element_type=F32)
    small_ref[...] = acc


def _rmsnorm_small(x2d, g, w_small):
    m, d = x2d.shape
    tr = _tile(m, 256)
    wh = _round_bf16(w_small)
    wl = (w_small - wh).astype(BF16)
    wh = wh.astype(BF16)
    return pl.pallas_call(
        _rms_small_kernel,
        grid=(m // tr,),
        in_specs=[pl.BlockSpec((tr, d), lambda i: (i, 0)),
                  pl.BlockSpec((1, d), lambda i: (0, 0)),
                  pl.BlockSpec((d, SMALL_W), lambda i: (0, 0)),
                  pl.BlockSpec((d, SMALL_W), lambda i: (0, 0))],
        out_specs=[pl.BlockSpec((tr, d), lambda i: (i, 0)),
                   pl.BlockSpec((tr, SMALL_W), lambda i: (i, 0))],
        out_shape=[jax.ShapeDtypeStruct((m, d), BF16),
                   jax.ShapeDtypeStruct((m, SMALL_W), F32)],
        compiler_params=_cparams(("parallel",)),
        name="rmsnorm_small_proj",
    )(x2d, g.reshape(1, d), wh, wl)


def _mm_kernel(*refs, nk, epilogue):
    if epilogue == "residual":
        a_ref, w_ref, r_ref, o_ref = refs[:4]
        scratch = refs[4:]
    else:
        a_ref, w_ref, o_ref = refs[:3]
        scratch = refs[3:]

    def finish(acc):
        if epilogue == "relu2":
            acc = jnp.square(jnp.maximum(acc, 0.0))
        elif epilogue == "residual":
            acc = acc + r_ref[...]
        o_ref[...] = acc.astype(o_ref.dtype)

    part = jnp.dot(a_ref[...], w_ref[...], preferred_element_type=F32)
    if nk == 1:
        finish(part)
        return
    acc_ref = scratch[0]
    k = pl.program_id(2)

    @pl.when(k == 0)
    def _():
        acc_ref[...] = part

    @pl.when(jnp.logical_and(k > 0, k < nk - 1))
    def _():
        acc_ref[...] += part

    @pl.when(k == nk - 1)
    def _():
        finish(acc_ref[...] + part)


def _matmul(a, w, *, out_dtype, epilogue="plain", residual=None, tm=1024, tn=1024, tk=4096,
            name="matmul"):
    m, kdim = a.shape
    n = w.shape[1]
    tm, tn, tk = _tile(m, tm), _tile(n, tn), _tile(kdim, tk)
    nk = kdim // tk
    in_specs = [pl.BlockSpec((tm, tk), lambda i, j, k: (i, k)),
                pl.BlockSpec((tk, tn), lambda i, j, k: (k, j))]
    args = [a, w]
    if epilogue == "residual":
        in_specs.append(pl.BlockSpec((tm, tn), lambda i, j, k: (i, j)))
        args.append(residual)
    scratch = [pltpu.VMEM((tm, tn), F32)] if nk > 1 else []
    return pl.pallas_call(
        functools.partial(_mm_kernel, nk=nk, epilogue=epilogue),
        grid=(m // tm, n // tn, nk),
        in_specs=in_specs,
        out_specs=pl.BlockSpec((tm, tn), lambda i, j, k: (i, j)),
        out_shape=jax.ShapeDtypeStruct((m, n), out_dtype),
        scratch_shapes=scratch,
        compiler_params=_cparams(("parallel", "parallel", "arbitrary")),
        name=name,
    )(*args)


def _merge_kernel(ya_ref, yb_ref, wa_ref, wb_ref, ga_ref, gb_ref, o_ref):
    pa = jnp.dot(ya_ref[...], wa_ref[...], preferred_element_type=F32)
    pb = jnp.dot(yb_ref[...], wb_ref[...], preferred_element_type=F32)
    ga = jax.nn.sigmoid(ga_ref[...].astype(F32))
    gb = jax.nn.sigmoid(gb_ref[...].astype(F32))
    o_ref[...] = (ga * pa + gb * pb).astype(o_ref.dtype)


def _gated_merge(ya, yb, wa, wb, proj, ga_col, gb_col, d_model):
    m, e = ya.shape
    tm, tn = _tile(m, 1024), _tile(d_model, 512)
    assert ga_col % tn == 0 and gb_col % tn == 0
    ga_blk, gb_blk = ga_col // tn, gb_col // tn
    return pl.pallas_call(
        _merge_kernel,
        grid=(m // tm, d_model // tn),
        in_specs=[pl.BlockSpec((tm, e), lambda i, j: (i, 0)),
                  pl.BlockSpec((tm, e), lambda i, j: (i, 0)),
                  pl.BlockSpec((e, tn), lambda i, j: (0, j)),
                  pl.BlockSpec((e, tn), lambda i, j: (0, j)),
                  pl.BlockSpec((tm, tn), lambda i, j: (i, ga_blk + j)),
                  pl.BlockSpec((tm, tn), lambda i, j: (i, gb_blk + j))],
        out_specs=pl.BlockSpec((tm, tn), lambda i, j: (i, j)),
        out_shape=jax.ShapeDtypeStruct((m, d_model), BF16),
        compiler_params=_cparams(("parallel", "parallel")),
        name="gated_merge",
    )(ya, yb, wa, wb, proj, proj)


def _indexer_kernel(ik_ref, qt_ref, w_ref, mask_ref, key_ref, gmax_ref, *, top_k, seq):
    tq = IDX_Q_TILE
    qi = pl.program_id(1)
    q0 = qi * tq
    n_rows = q0 + tq

    t_idx = q0 + lax.broadcasted_iota(jnp.int32, (IDX_KEY_TILE, tq), 1)
    limit = (t_idx // CHUNK + 1) * CHUNK
    s_loc = lax.broadcasted_iota(jnp.int32, (IDX_KEY_TILE, tq), 0)

    def score_body(j, carry):
        row0 = pl.multiple_of(j * IDX_KEY_TILE, IDX_KEY_TILE)
        k_tile = ik_ref[0, pl.ds(row0, IDX_KEY_TILE), :]
        acc = jnp.zeros((IDX_KEY_TILE, tq), F32)
        for h in range(N_IDX_HEADS):
            sc = jnp.dot(k_tile, qt_ref[0, 0, :, h * tq:(h + 1) * tq],
                         preferred_element_type=F32)
            acc = acc + jnp.maximum(sc, 0.0) * w_ref[0, 0, :, h * tq:(h + 1) * tq]
        bits = pltpu.bitcast(acc, jnp.int32)
        key = bits ^ ((bits >> 31) & jnp.int32(0x7FFFFFFF))
        key = jnp.where(row0 + s_loc < limit, key, jnp.int32(INT_MIN))
        key_ref[pl.ds(row0, IDX_KEY_TILE), :] = key
        return carry

    lax.fori_loop(0, n_rows // IDX_KEY_TILE, score_body, 0)

    rc = IDX_Q_TILE
    n_blk = n_rows // rc

    def count_ge(cand):
        def body(r, cnt):
            blk = key_ref[pl.ds(pl.multiple_of(r * rc, rc), rc), :]
            ones = jnp.where(blk >= cand, jnp.int32(1), jnp.int32(0))
            return cnt + ones.reshape(rc // 8, 8, tq).sum(axis=0)
        cnt = lax.fori_loop(0, n_blk, body, jnp.zeros((8, tq), jnp.int32))
        return cnt.sum(axis=0, keepdims=True)

    assert rc >= top_k
    gmax_ref[...] = key_ref[pl.ds(0, rc), :]

    def bounds_body(r, carry):
        gmax_ref[...] = jnp.maximum(gmax_ref[...], key_ref[pl.ds(pl.multiple_of(r * rc, rc), rc), :])
        return carry

    lax.fori_loop(1, n_blk, bounds_body, 0)
    gmax = gmax_ref[...]
    lo = jnp.min(gmax, axis=0, keepdims=True)
    hi = jnp.minimum(jnp.max(gmax, axis=0, keepdims=True), jnp.int32(2 ** 31 - 2)) + 1

    def n_open(lo, hi, cnt_lo):
        is_open = jnp.where(cnt_lo != top_k, jnp.where(hi - lo != 1, jnp.int32(1), jnp.int32(0)), jnp.int32(0))
        return jnp.max(is_open)

    def search_body(c):
        lo, hi, cnt_lo, _ = c
        for _ in range(SEARCH_STEPS_PER_CHECK):
            mid = lo + lax.shift_right_logical(hi - lo, jnp.int32(1))
            cnt = count_ge(mid)
            ge = cnt >= top_k
            lo, cnt_lo, hi = jnp.where(ge, mid, lo), jnp.where(ge, cnt, cnt_lo), jnp.where(ge, hi, mid)
        return lo, hi, cnt_lo, n_open(lo, hi, cnt_lo)

    cnt_lo = jnp.full((1, tq), top_k + 1, jnp.int32)
    thr, _, _, _ = lax.while_loop(lambda c: c[3] > 0, search_body, (lo, hi, cnt_lo, n_open(lo, hi, cnt_lo)))
    thr = jnp.maximum(thr, jnp.int32(INT_MIN + 1))

    def emit_body(r, carry):
        row0 = pl.multiple_of(r * rc, rc)
        blk = key_ref[pl.ds(row0, rc), :]
        mask_ref[0, pl.ds(row0, rc), :] = jnp.where(blk >= thr, jnp.int32(1), jnp.int32(0)).astype(jnp.int8)
        return carry

    lax.fori_loop(0, n_blk, emit_body, 0)

    def zero_body(r, carry):
        row0 = pl.multiple_of(r * rc, rc)
        mask_ref[0, pl.ds(row0, rc), :] = jnp.zeros((rc, tq), jnp.int8)
        return carry

    lax.fori_loop(n_blk, seq // rc, zero_body, 0)


def _indexer_mask(ik, qt, w_row, top_k):
    b, s, _ = ik.shape
    tq = IDX_Q_TILE
    n_q = s // tq
    return pl.pallas_call(
        functools.partial(_indexer_kernel, top_k=top_k, seq=s),
        grid=(b, n_q),
        in_specs=[pl.BlockSpec((1, s, IDX_DIM), lambda bi, qi: (bi, 0, 0)),
                  pl.BlockSpec((1, 1, IDX_DIM, N_IDX_HEADS * tq), lambda bi, qi: (bi, qi, 0, 0)),
                  pl.BlockSpec((1, 1, 1, N_IDX_HEADS * tq), lambda bi, qi: (bi, qi, 0, 0))],
        out_specs=pl.BlockSpec((1, s, tq), lambda bi, qi: (bi, 0, qi)),
        out_shape=jax.ShapeDtypeStruct((b, s, s), jnp.int8),
        scratch_shapes=[pltpu.VMEM((s, tq), jnp.int32), pltpu.VMEM((tq, tq), jnp.int32)],
        compiler_params=_cparams(("parallel", "arbitrary")),
        name="dsa_indexer_topk_mask",
    )(ik, qt, w_row)


def _qk(k_tile, q_tile):
    return lax.dot_general(k_tile, q_tile, (((1,), (1,)), ((), ())), preferred_element_type=F32)


def _chain_update(g, s, m, acc_ref, vt_tile, *, guard, shift=None):
    m_tile = jnp.max(s, axis=0, keepdims=True)
    if shift is not None:
        m_tile = m_tile + shift
    m_new = jnp.maximum(m, m_tile)
    m_ref = jnp.where(m_new == NEG_INF, 0.0, m_new) if guard else m_new
    p = jnp.exp2(s - (m_ref if shift is None else m_ref - shift))
    alpha = jnp.exp2(m - m_ref)
    acc_ref[g] = acc_ref[g] * alpha + jnp.dot(vt_tile, p.astype(BF16), preferred_element_type=F32)
    return m_new


def _run_chains(ms, s0_ref, acc_ref, qk, vt, *, adjust=None, guard, prefetch):
    s_next = s0_ref[...]
    out = []
    for g in range(ATT_HEADS):
        s, shift = s_next, None
        if g + 1 < ATT_HEADS:
            s_next = qk(g + 1)
        elif prefetch is not None:
            s0_ref[...] = prefetch()
        if adjust is not None:
            s, shift = adjust(g, s)
        out.append(_chain_update(g, s, ms[g], acc_ref, vt(g), guard=guard, shift=shift))
    return tuple(out)


def _chains_init(acc_ref, tq):
    acc_ref[...] = jnp.zeros(acc_ref.shape, F32)
    return tuple(jnp.full((1, tq), NEG_INF, F32) for _ in range(ATT_HEADS))


def _chains_finish(acc_ref, o_ref):
    for g in range(ATT_HEADS):
        acc = acc_ref[g]
        out = acc[:HEAD_DIM] / acc[HEAD_DIM:HEAD_DIM + 1]
        o_ref[0, :, g * HEAD_DIM:(g + 1) * HEAD_DIM] = out.T.astype(o_ref.dtype)


def _dsa_attn_kernel(q_ref, k_ref, vt_ref, mask_ref, bias_ref, o_ref, acc_ref, s0_ref):
    tq = tk = ATT_TILE
    nb = tk // BIAS_TILE
    qi = pl.program_id(2)

    def qk(g, j):
        row0 = pl.multiple_of(j * tk, tk)
        cols = slice(g * HEAD_DIM, (g + 1) * HEAD_DIM)
        return _qk(k_ref[0, pl.ds(row0, tk), cols], q_ref[0, :, cols])

    def step(j, ms, far):
        row0 = pl.multiple_of(j * tk, tk)
        sel = mask_ref[0, pl.ds(row0, tk), :].astype(jnp.int32) != 0

        def adjust(g, s):
            if far:
                far_row = bias_ref[g, N_BIAS_OFFSETS - 1, 0:1, :]
                return jnp.where(sel, s, NEG_INF), jnp.concatenate([far_row] * nb, axis=1)
            rows = []
            for a in range(nb):
                rows.append(jnp.concatenate(
                    [bias_ref[g, jnp.minimum(1 + nb * (qi - j) - a + bb, N_BIAS_OFFSETS - 1)]
                     for bb in range(nb)], axis=1))
            return jnp.where(sel, s + jnp.concatenate(rows, axis=0), NEG_INF), None

        return _run_chains(ms, s0_ref, acc_ref, lambda g: qk(g, j), lambda g: vt_ref[0, g, j],
                           adjust=adjust, guard=True, prefetch=lambda: qk(0, jnp.minimum(j + 1, qi)))

    far_dist = -(-(N_BIAS_OFFSETS + nb - 3) // nb)
    n_far = jnp.maximum(qi - far_dist + 1, 0)
    s0_ref[...] = qk(0, 0)
    ms = lax.fori_loop(0, n_far, lambda j, ms: step(j, ms, True), _chains_init(acc_ref, tq))
    lax.fori_loop(n_far, qi + 1, lambda j, ms: step(j, ms, False), ms)
    _chains_finish(acc_ref, o_ref)


def _dsa_attention(proj3, q_blk0, k_blk0, vt, mask, bias):
    b, s, _ = proj3.shape
    h = N_HEADS
    tq, g = ATT_TILE, ATT_HEADS
    assert q_blk0 % g == 0 and k_blk0 % g == 0 and h % g == 0
    once = pl.Buffered(1)
    return pl.pallas_call(
        _dsa_attn_kernel,
        grid=(b, h // g, s // tq),
        in_specs=[pl.BlockSpec((1, tq, g * HEAD_DIM), lambda bi, hi, qi: (bi, qi, q_blk0 // g + hi)),
                  pl.BlockSpec((1, s, g * HEAD_DIM), lambda bi, hi, qi: (bi, 0, k_blk0 // g + hi),
                               pipeline_mode=once),
                  pl.BlockSpec((1, g, s // tq, ATT_VROWS, tq), lambda bi, hi, qi: (bi, hi, 0, 0, 0),
                               pipeline_mode=once),
                  pl.BlockSpec((1, s, tq), lambda bi, hi, qi: (bi, 0, qi)),
                  pl.BlockSpec((g, N_BIAS_OFFSETS, BIAS_TILE, BIAS_TILE),
                               lambda bi, hi, qi: (hi, 0, 0, 0), pipeline_mode=once)],
        out_specs=pl.BlockSpec((1, tq, g * HEAD_DIM), lambda bi, hi, qi: (bi, qi, hi)),
        out_shape=jax.ShapeDtypeStruct((b, s, h * HEAD_DIM), BF16),
        scratch_shapes=[pltpu.VMEM((g, ATT_VROWS, tq), F32), pltpu.VMEM((tq, tq), F32)],
        compiler_params=_cparams(("parallel", "parallel", "arbitrary")),
        name="dsa_masked_attention",
    )(proj3, proj3, vt, mask, bias)


def _fox_attn_kernel(q_ref, k_ref, vt_ref, o_ref, acc_ref, s0_ref):
    tq = tk = ATT_TILE
    qi = pl.program_id(2)

    def qk(g, j):
        row0 = pl.multiple_of(j * tk, tk)
        cols = slice(g * ATT_KDIM, (g + 1) * ATT_KDIM)
        return _qk(k_ref[0, pl.ds(row0, tk), cols], q_ref[0, :, cols])

    def body(j, ms):
        return _run_chains(ms, s0_ref, acc_ref, lambda g: qk(g, j), lambda g: vt_ref[0, g, j],
                           guard=False, prefetch=lambda: qk(0, j + 1))

    s0_ref[...] = qk(0, 0)
    ms = lax.fori_loop(0, qi, body, _chains_init(acc_ref, tq))
    row = lax.broadcasted_iota(jnp.int32, (tk, tq), 0)
    col = lax.broadcasted_iota(jnp.int32, (tk, tq), 1)
    visible = row <= col
    _run_chains(ms, s0_ref, acc_ref, lambda g: qk(g, qi), lambda g: vt_ref[0, g, qi],
                adjust=lambda g, s: (jnp.where(visible, s, NEG_INF), None), guard=False, prefetch=None)
    _chains_finish(acc_ref, o_ref)


def _fox_attention(q_aug, k_aug, vt):
    b, s, _ = q_aug.shape
    h = N_HEADS
    tq, g = ATT_TILE, ATT_HEADS
    once = pl.Buffered(1)
    return pl.pallas_call(
        _fox_attn_kernel,
        grid=(b, h // g, s // tq),
        in_specs=[pl.BlockSpec((1, tq, g * ATT_KDIM), lambda bi, hi, qi: (bi, qi, hi)),
                  pl.BlockSpec((1, s, g * ATT_KDIM), lambda bi, hi, qi: (bi, 0, hi), pipeline_mode=once),
                  pl.BlockSpec((1, g, s // tq, ATT_VROWS, tq), lambda bi, hi, qi: (bi, hi, 0, 0, 0),
                               pipeline_mode=once)],
        out_specs=pl.BlockSpec((1, tq, g * HEAD_DIM), lambda bi, hi, qi: (bi, qi, hi)),
        out_shape=jax.ShapeDtypeStruct((b, s, h * HEAD_DIM), BF16),
        scratch_shapes=[pltpu.VMEM((g, ATT_VROWS, tq), F32), pltpu.VMEM((tq, tq), F32)],
        compiler_params=_cparams(("parallel", "parallel", "arbitrary")),
        name="fox_attention",
    )(q_aug, k_aug, vt)


def _t5_bucket(rel):
    half = N_BUCKETS // 2
    max_exact = half // 2
    base = jnp.where(rel > 0, half, 0)
    n = jnp.abs(rel)
    nf = jnp.maximum(n, max_exact).astype(jnp.float32)
    large = max_exact + (jnp.log(nf / max_exact) / math.log(MAX_DISTANCE / max_exact)
                         * (half - max_exact)).astype(jnp.int32)
    large = jnp.minimum(large, half - 1)
    return base + jnp.where(n < max_exact, n, large)


def _bias_tiles(rel_bias):
    t = BIAS_TILE
    half, max_exact = N_BUCKETS // 2, N_BUCKETS // 4
    last_bucket_start = max_exact * (MAX_DISTANCE / max_exact) ** ((half - 1 - max_exact) / (half - max_exact))
    assert (N_BIAS_OFFSETS - 2) * t - (t - 1) > last_bucket_start + 1
    i = jnp.arange(t, dtype=jnp.int32)[:, None]
    j = jnp.arange(t, dtype=jnp.int32)[None, :]
    rel = jnp.stack([i - j + (1 - o) * t for o in range(N_BIAS_OFFSETS)])
    onehot = jax.nn.one_hot(_t5_bucket(rel), N_BUCKETS, dtype=F32)
    return jnp.einsum("oijk,kh->hoij", onehot, rel_bias.astype(F32) * LOG2E,
                      precision=lax.Precision.HIGHEST)


def _values_t_tiled(a):
    b, s, _ = a.shape
    t = ATT_TILE
    vt = jnp.transpose(a.reshape(b, s // t, t, N_HEADS, HEAD_DIM), (0, 3, 1, 4, 2))
    ones = jnp.ones((b, N_HEADS, s // t, ATT_VROWS - HEAD_DIM, t), a.dtype)
    return jnp.concatenate([vt, ones], axis=3)


def _augment(a, extra):
    b, s, _ = a.shape
    pad = jnp.zeros((b, s, N_HEADS, ATT_KDIM - HEAD_DIM - extra.shape[-1]), a.dtype)
    out = jnp.concatenate([a.reshape(b, s, N_HEADS, HEAD_DIM), extra, pad], axis=-1)
    return out.reshape(b, s, N_HEADS * ATT_KDIM)


def _round_bf16(x):
    return lax.reduce_precision(x, exponent_bits=8, mantissa_bits=7)


def _split3(c):
    hi = _round_bf16(c)
    r = c - hi
    mid = _round_bf16(r)
    lo = r - mid
    return hi.astype(BF16), mid.astype(BF16), lo.astype(BF16)


def kernel(x, w_in, b_f, w_out_a, w_out_b, w_o, w_up, w_down, g_mix, g_mlp, g_final, rel_bias):
    b, s, d = x.shape
    m = b * s
    depth = w_in.shape[0]
    assert s % ATT_TILE == 0 and ATT_TILE % BIAS_TILE == 0 and IDX_Q_TILE % CHUNK == 0
    top_k = min(TOPK_MAX, s // 4)

    sizes = (D_HEADS, D_HEADS, D_HEADS, N_IDX_HEADS * IDX_DIM, IDX_DIM, N_IDX_HEADS,
             D_HEADS, D_HEADS, D_HEADS, N_HEADS, d, d)
    offs = [0]
    for sz in sizes:
        offs.append(offs[-1] + sz)
    (o_qa, o_ka, o_va, o_iq, o_ik, o_iw, o_qb, o_kb, o_vb, o_fl, o_ga, o_gb, _) = offs

    p_qa, p_ka, p_va = 0, D_HEADS, 2 * D_HEADS
    p_iq = 3 * D_HEADS
    p_qb = p_iq + N_IDX_HEADS * IDX_DIM
    p_kb, p_vb = p_qb + D_HEADS, p_qb + 2 * D_HEADS
    p_ga = p_qb + 3 * D_HEADS
    p_gb = p_ga + d
    n_proj = p_gb + d

    q_scale = HEAD_DIM ** -0.5 * LOG2E
    bias = _bias_tiles(rel_bias)

    x2 = x.reshape(m, d)
    for l in range(depth):
        w = w_in[l]

        def cols(o, n):
            return lax.slice_in_dim(w, o, o + n, axis=1)

        w_main = jnp.concatenate([
            cols(o_qa, D_HEADS) * q_scale, cols(o_ka, 2 * D_HEADS),
            cols(o_iq, N_IDX_HEADS * IDX_DIM),
            cols(o_qb, D_HEADS) * q_scale, cols(o_kb, 2 * D_HEADS),
            cols(o_ga, 2 * d)], axis=1).astype(BF16)
        w_small = jnp.concatenate([
            cols(o_ik, IDX_DIM) * (IDX_DIM ** -0.5),
            cols(o_iw, N_IDX_HEADS) * (N_IDX_HEADS ** -0.5),
            cols(o_fl, N_HEADS),
            jnp.zeros((d, SMALL_W - IDX_DIM - N_IDX_HEADS - N_HEADS), F32)], axis=1)

        xn, small = _rmsnorm_small(x2, g_mix[l], w_small)
        proj = _matmul(xn, w_main, out_dtype=BF16, name="in_proj")
        proj3 = proj.reshape(b, s, n_proj)

        tq = IDX_Q_TILE
        n_q = s // tq
        ik = small[:, :IDX_DIM].astype(BF16).reshape(b, s, IDX_DIM)
        iq = lax.slice_in_dim(proj3, p_iq, p_iq + N_IDX_HEADS * IDX_DIM, axis=2)
        iq_t = jnp.transpose(iq.reshape(b, n_q, tq, N_IDX_HEADS, IDX_DIM), (0, 1, 4, 3, 2))
        iq_t = iq_t.reshape(b, n_q, IDX_DIM, N_IDX_HEADS * tq)
        iw = small[:, IDX_DIM:IDX_DIM + N_IDX_HEADS].reshape(b, n_q, tq, N_IDX_HEADS)
        iw_row = jnp.transpose(iw, (0, 1, 3, 2)).reshape(b, n_q, 1, N_IDX_HEADS * tq)
        mask = _indexer_mask(ik, iq_t, iw_row, top_k)

        va_t = _values_t_tiled(lax.slice_in_dim(proj3, p_va, p_va + D_HEADS, axis=2))
        ya = _dsa_attention(proj3, p_qa // HEAD_DIM, p_ka // HEAD_DIM, va_t, mask, bias)

        fl = small[:, IDX_DIM + N_IDX_HEADS:IDX_DIM + N_IDX_HEADS + N_HEADS].reshape(b, s, N_HEADS)
        log_f = jax.nn.log_sigmoid(fl + b_f[l].astype(F32))
        cum = jnp.cumsum(log_f, axis=1) * LOG2E
        c_hi, c_mid, c_lo = _split3(cum)
        ones = jnp.ones((b, s, N_HEADS), BF16)
        k_aug = _augment(lax.slice_in_dim(proj3, p_kb, p_kb + D_HEADS, axis=2),
                         jnp.stack([c_hi, c_mid, c_lo, ones, ones, ones], axis=-1))
        q_aug = _augment(lax.slice_in_dim(proj3, p_qb, p_qb + D_HEADS, axis=2),
                         jnp.stack([-ones, -ones, -ones, c_hi, c_mid, c_lo], axis=-1))
        vb_t = _values_t_tiled(lax.slice_in_dim(proj3, p_vb, p_vb + D_HEADS, axis=2))
        yb = _fox_attention(q_aug, k_aug, vb_t)

        merged = _gated_merge(ya.reshape(m, D_HEADS), yb.reshape(m, D_HEADS),
                              w_out_a[l].astype(BF16), w_out_b[l].astype(BF16),
                              proj, p_ga, p_gb, d)
        x2 = _matmul(merged, w_o[l].astype(BF16), out_dtype=F32, epilogue="residual",
                     residual=x2, tn=512, name="out_proj_residual")

        hn = _rmsnorm(x2, g_mlp[l], BF16)
        hid = _matmul(hn, w_up[l].astype(BF16), out_dtype=BF16, epilogue="relu2", name="mlp_up_relu2")
        x2 = _matmul(hid, w_down[l].astype(BF16), out_dtype=F32, epilogue="residual",
                     residual=x2, tk=2048, name="mlp_down_residual")

    return _rmsnorm(x2, g_final, F32).reshape(b, s, d)
```

```python
import functools
import math

import jax
import jax.numpy as jnp
from jax import lax
from jax.experimental import pallas as pl
from jax.experimental.pallas import tpu as pltpu

F32 = jnp.float32
BF16 = jnp.bfloat16

HEAD_DIM = 128
N_HEADS = 16
D_HEADS = N_HEADS * HEAD_DIM
N_IDX_HEADS = 32
IDX_DIM = 64
CHUNK = 64
TOPK_MAX = 256
N_BUCKETS = 32
MAX_DISTANCE = 1024
EPS = 1e-6

SMALL_W = 128
INT_MIN = -2 ** 31
NEG_INF = float("-inf")

VMEM_LIMIT_BYTES = 56 * 1024 * 1024

ATT_TILE = 512
ATT_HEADS = 4
ATT_KDIM = 256
ATT_VROWS = HEAD_DIM + 16
IDX_Q_TILE = 256
IDX_KEY_TILE = 128
SEARCH_STEPS_PER_CHECK = 3
BIAS_TILE = 256
N_BIAS_OFFSETS = 6
LOG2E = math.log2(math.e)


def _cparams(sem):
    return pltpu.CompilerParams(dimension_semantics=sem, vmem_limit_bytes=VMEM_LIMIT_BYTES)


def _tile(n, pref):
    t = min(n, pref)
    assert n % t == 0, (n, pref)
    return t


def _rms_kernel(x_ref, g_ref, o_ref):
    x = x_ref[...]
    y = x * lax.rsqrt(jnp.mean(x * x, axis=-1, keepdims=True) + EPS)
    o_ref[...] = (y * g_ref[...]).astype(o_ref.dtype)


def _rmsnorm(x2d, g, out_dtype):
    m, d = x2d.shape
    tr = _tile(m, 256)
    return pl.pallas_call(
        _rms_kernel,
        grid=(m // tr,),
        in_specs=[pl.BlockSpec((tr, d), lambda i: (i, 0)),
                  pl.BlockSpec((1, d), lambda i: (0, 0))],
        out_specs=pl.BlockSpec((tr, d), lambda i: (i, 0)),
        out_shape=jax.ShapeDtypeStruct((m, d), out_dtype),
        compiler_params=_cparams(("parallel",)),
        name="rmsnorm",
    )(x2d, g.reshape(1, d))


def _rms_small_kernel(x_ref, g_ref, wh_ref, wl_ref, xn_ref, small_ref):
    x = x_ref[...]
    y = x * lax.rsqrt(jnp.mean(x * x, axis=-1, keepdims=True) + EPS) * g_ref[...]
    yh = y.astype(BF16)
    xn_ref[...] = yh
    yl = (y - yh.astype(F32)).astype(BF16)
    wh = wh_ref[...]
    acc = jnp.dot(yh, wh, preferred_element_type=F32)
    acc = acc + jnp.dot(yl, wh, preferred_element_type=F32)
    acc = acc + jnp.dot(yh, wl_ref[...], preferred_element_type=F32)
    small_ref[...] = acc


def _rmsnorm_small(x2d, g, w_small):
    m, d = x2d.shape
    tr = _tile(m, 256)
    wh = _round_bf16(w_small)
    wl = (w_small - wh).astype(BF16)
    wh = wh.astype(BF16)
    return pl.pallas_call(
        _rms_small_kernel,
        grid=(m // tr,),
        in_specs=[pl.BlockSpec((tr, d), lambda i: (i, 0)),
                  pl.BlockSpec((1, d), lambda i: (0, 0)),
                  pl.BlockSpec((d, SMALL_W), lambda i: (0, 0)),
                  pl.BlockSpec((d, SMALL_W), lambda i: (0, 0))],
        out_specs=[pl.BlockSpec((tr, d), lambda i: (i, 0)),
                   pl.BlockSpec((tr, SMALL_W), lambda i: (i, 0))],
        out_shape=[jax.ShapeDtypeStruct((m, d), BF16),
                   jax.ShapeDtypeStruct((m, SMALL_W), F32)],
        compiler_params=_cparams(("parallel",)),
        name="rmsnorm_small_proj",
    )(x2d, g.reshape(1, d), wh, wl)


def _mm_kernel(*refs, nk, epilogue):
    if epilogue == "residual":
        a_ref, w_ref, r_ref, o_ref = refs[:4]
        scratch = refs[4:]
    else:
        a_ref, w_ref, o_ref = refs[:3]
        scratch = refs[3:]

    def finish(acc):
        if epilogue == "relu2":
            acc = jnp.square(jnp.maximum(acc, 0.0))
        elif epilogue == "residual":
            acc = acc + r_ref[...]
        o_ref[...] = acc.astype(o_ref.dtype)

    part = jnp.dot(a_ref[...], w_ref[...], preferred_element_type=F32)
    if nk == 1:
        finish(part)
        return
    acc_ref = scratch[0]
    k = pl.program_id(2)

    @pl.when(k == 0)
    def _():
        acc_ref[...] = part

    @pl.when(jnp.logical_and(k > 0, k < nk - 1))
    def _():
        acc_ref[...] += part

    @pl.when(k == nk - 1)
    def _():
        finish(acc_ref[...] + part)


def _matmul(a, w, *, out_dtype, epilogue="plain", residual=None, tm=1024, tn=1024, tk=4096,
            name="matmul"):
    m, kdim = a.shape
    n = w.shape[1]
    tm, tn, tk = _tile(m, tm), _tile(n, tn), _tile(kdim, tk)
    nk = kdim // tk
    in_specs = [pl.BlockSpec((tm, tk), lambda i, j, k: (i, k)),
                pl.BlockSpec((tk, tn), lambda i, j, k: (k, j))]
    args = [a, w]
    if epilogue == "residual":
        in_specs.append(pl.BlockSpec((tm, tn), lambda i, j, k: (i, j)))
        args.append(residual)
    scratch = [pltpu.VMEM((tm, tn), F32)] if nk > 1 else []
    return pl.pallas_call(
        functools.partial(_mm_kernel, nk=nk, epilogue=epilogue),
        grid=(m // tm, n // tn, nk),
        in_specs=in_specs,
        out_specs=pl.BlockSpec((tm, tn), lambda i, j, k: (i, j)),
        out_shape=jax.ShapeDtypeStruct((m, n), out_dtype),
        scratch_shapes=scratch,
        compiler_params=_cparams(("parallel", "parallel", "arbitrary")),
        name=name,
    )(*args)


def _merge_kernel(ya_ref, yb_ref, wa_ref, wb_ref, ga_ref, gb_ref, o_ref):
    pa = jnp.dot(ya_ref[...], wa_ref[...], preferred_element_type=F32)
    pb = jnp.dot(yb_ref[...], wb_ref[...], preferred_element_type=F32)
    ga = jax.nn.sigmoid(ga_ref[...].astype(F32))
    gb = jax.nn.sigmoid(gb_ref[...].astype(F32))
    o_ref[...] = (ga * pa + gb * pb).astype(o_ref.dtype)


def _gated_merge(ya, yb, wa, wb, proj, ga_col, gb_col, d_model):
    m, e = ya.shape
    tm, tn = _tile(m, 1024), _tile(d_model, 512)
    assert ga_col % tn == 0 and gb_col % tn == 0
    ga_blk, gb_blk = ga_col // tn, gb_col // tn
    return pl.pallas_call(
        _merge_kernel,
        grid=(m // tm, d_model // tn),
        in_specs=[pl.BlockSpec((tm, e), lambda i, j: (i, 0)),
                  pl.BlockSpec((tm, e), lambda i, j: (i, 0)),
                  pl.BlockSpec((e, tn), lambda i, j: (0, j)),
                  pl.BlockSpec((e, tn), lambda i, j: (0, j)),
                  pl.BlockSpec((tm, tn), lambda i, j: (i, ga_blk + j)),
                  pl.BlockSpec((tm, tn), lambda i, j: (i, gb_blk + j))],
        out_specs=pl.BlockSpec((tm, tn), lambda i, j: (i, j)),
        out_shape=jax.ShapeDtypeStruct((m, d_model), BF16),
        compiler_params=_cparams(("parallel", "parallel")),
        name="gated_merge",
    )(ya, yb, wa, wb, proj, proj)


def _indexer_kernel(ik_ref, qt_ref, w_ref, mask_ref, key_ref, gmax_ref, *, top_k, seq):
    tq = IDX_Q_TILE
    qi = pl.program_id(1)
    q0 = qi * tq
    n_rows = q0 + tq

    t_idx = q0 + lax.broadcasted_iota(jnp.int32, (IDX_KEY_TILE, tq), 1)
    limit = (t_idx // CHUNK + 1) * CHUNK
    s_loc = lax.broadcasted_iota(jnp.int32, (IDX_KEY_TILE, tq), 0)

    def score_body(j, carry):
        row0 = pl.multiple_of(j * IDX_KEY_TILE, IDX_KEY_TILE)
        k_tile = ik_ref[0, pl.ds(row0, IDX_KEY_TILE), :]
        acc = jnp.zeros((IDX_KEY_TILE, tq), F32)
        for h in range(N_IDX_HEADS):
            sc = jnp.dot(k_tile, qt_ref[0, 0, :, h * tq:(h + 1) * tq],
                         preferred_element_type=F32)
            acc = acc + jnp.maximum(sc, 0.0) * w_ref[0, 0, :, h * tq:(h + 1) * tq]
        bits = pltpu.bitcast(acc, jnp.int32)
        key = bits ^ ((bits >> 31) & jnp.int32(0x7FFFFFFF))
        key = jnp.where(row0 + s_loc < limit, key, jnp.int32(INT_MIN))
        key_ref[pl.ds(row0, IDX_KEY_TILE), :] = key
        return carry

    lax.fori_loop(0, n_rows // IDX_KEY_TILE, score_body, 0)

    rc = IDX_Q_TILE
    n_blk = n_rows // rc

    def count_ge(cand):
        def body(r, cnt):
            blk = key_ref[pl.ds(pl.multiple_of(r * rc, rc), rc), :]
            ones = jnp.where(blk >= cand, jnp.int32(1), jnp.int32(0))
            return cnt + ones.reshape(rc // 8, 8, tq).sum(axis=0)
        cnt = lax.fori_loop(0, n_blk, body, jnp.zeros((8, tq), jnp.int32))
        return cnt.sum(axis=0, keepdims=True)

    assert rc >= top_k
    gmax_ref[...] = key_ref[pl.ds(0, rc), :]

    def bounds_body(r, carry):
        gmax_ref[...] = jnp.maximum(gmax_ref[...], key_ref[pl.ds(pl.multiple_of(r * rc, rc), rc), :])
        return carry

    lax.fori_loop(1, n_blk, bounds_body, 0)
    gmax = gmax_ref[...]
    lo = jnp.min(gmax, axis=0, keepdims=True)
    hi = jnp.minimum(jnp.max(gmax, axis=0, keepdims=True), jnp.int32(2 ** 31 - 2)) + 1

    def n_open(lo, hi, cnt_lo):
        is_open = jnp.where(cnt_lo != top_k, jnp.where(hi - lo != 1, jnp.int32(1), jnp.int32(0)), jnp.int32(0))
        return jnp.max(is_open)

    def search_body(c):
        lo, hi, cnt_lo, _ = c
        for _ in range(SEARCH_STEPS_PER_CHECK):
            mid = lo + lax.shift_right_logical(hi - lo, jnp.int32(1))
            cnt = count_ge(mid)
            ge = cnt >= top_k
            lo, cnt_lo, hi = jnp.where(ge, mid, lo), jnp.where(ge, cnt, cnt_lo), jnp.where(ge, hi, mid)
        return lo, hi, cnt_lo, n_open(lo, hi, cnt_lo)

    cnt_lo = jnp.full((1, tq), top_k + 1, jnp.int32)
    thr, _, _, _ = lax.while_loop(lambda c: c[3] > 0, search_body, (lo, hi, cnt_lo, n_open(lo, hi, cnt_lo)))
    thr = jnp.maximum(thr, jnp.int32(INT_MIN + 1))

    def emit_body(r, carry):
        row0 = pl.multiple_of(r * rc, rc)
        blk = key_ref[pl.ds(row0, rc), :]
        mask_ref[0, pl.ds(row0, rc), :] = jnp.where(blk >= thr, jnp.int32(1), jnp.int32(0)).astype(jnp.int8)
        return carry

    lax.fori_loop(0, n_blk, emit_body, 0)

    def zero_body(r, carry):
        row0 = pl.multiple_of(r * rc, rc)
        mask_ref[0, pl.ds(row0, rc), :] = jnp.zeros((rc, tq), jnp.int8)
        return carry

    lax.fori_loop(n_blk, seq // rc, zero_body, 0)


def _indexer_mask(ik, qt, w_row, top_k):
    b, s, _ = ik.shape
    tq = IDX_Q_TILE
    n_q = s // tq
    return pl.pallas_call(
        functools.partial(_indexer_kernel, top_k=top_k, seq=s),
        grid=(b, n_q),
        in_specs=[pl.BlockSpec((1, s, IDX_DIM), lambda bi, qi: (bi, 0, 0)),
                  pl.BlockSpec((1, 1, IDX_DIM, N_IDX_HEADS * tq), lambda bi, qi: (bi, qi, 0, 0)),
                  pl.BlockSpec((1, 1, 1, N_IDX_HEADS * tq), lambda bi, qi: (bi, qi, 0, 0))],
        out_specs=pl.BlockSpec((1, s, tq), lambda bi, qi: (bi, 0, qi)),
        out_shape=jax.ShapeDtypeStruct((b, s, s), jnp.int8),
        scratch_shapes=[pltpu.VMEM((s, tq), jnp.int32), pltpu.VMEM((tq, tq), jnp.int32)],
        compiler_params=_cparams(("parallel", "arbitrary")),
        name="dsa_indexer_topk_mask",
    )(ik, qt, w_row)


def _qk(k_tile, q_tile):
    return lax.dot_general(k_tile, q_tile, (((1,), (1,)), ((), ())), preferred_element_type=F32)


def _chain_update(g, s, m, acc_ref, vt_tile, *, guard, shift=None):
    m_tile = jnp.max(s, axis=0, keepdims=True)
    if shift is not None:
        m_tile = m_tile + shift
    m_new = jnp.maximum(m, m_tile)
    m_ref = jnp.where(m_new == NEG_INF, 0.0, m_new) if guard else m_new
    p = jnp.exp2(s - (m_ref if shift is None else m_ref - shift))
    alpha = jnp.exp2(m - m_ref)
    acc_ref[g] = acc_ref[g] * alpha + jnp.dot(vt_tile, p.astype(BF16), preferred_element_type=F32)
    return m_new


def _run_chains(ms, s0_ref, acc_ref, qk, vt, *, adjust=None, guard, prefetch):
    s_next = s0_ref[...]
    out = []
    for g in range(ATT_HEADS):
        s, shift = s_next, None
        if g + 1 < ATT_HEADS:
            s_next = qk(g + 1)
        elif prefetch is not None:
            s0_ref[...] = prefetch()
        if adjust is not None:
            s, shift = adjust(g, s)
        out.append(_chain_update(g, s, ms[g], acc_ref, vt(g), guard=guard, shift=shift))
    return tuple(out)


def _chains_init(acc_ref, tq):
    acc_ref[...] = jnp.zeros(acc_ref.shape, F32)
    return tuple(jnp.full((1, tq), NEG_INF, F32) for _ in range(ATT_HEADS))


def _chains_finish(acc_ref, o_ref):
    for g in range(ATT_HEADS):
        acc = acc_ref[g]
        out = acc[:HEAD_DIM] / acc[HEAD_DIM:HEAD_DIM + 1]
        o_ref[0, :, g * HEAD_DIM:(g + 1) * HEAD_DIM] = out.T.astype(o_ref.dtype)


def _dsa_attn_kernel(q_ref, k_ref, vt_ref, mask_ref, bias_ref, o_ref, acc_ref, s0_ref):
    tq = tk = ATT_TILE
    nb = tk // BIAS_TILE
    qi = pl.program_id(2)

    def qk(g, j):
        row0 = pl.multiple_of(j * tk, tk)
        cols = slice(g * HEAD_DIM, (g + 1) * HEAD_DIM)
        return _qk(k_ref[0, pl.ds(row0, tk), cols], q_ref[0, :, cols])

    def step(j, ms, far):
        row0 = pl.multiple_of(j * tk, tk)
        sel = mask_ref[0, pl.ds(row0, tk), :].astype(jnp.int32) != 0

        def adjust(g, s):
            if far:
                far_row = bias_ref[g, N_BIAS_OFFSETS - 1, 0:1, :]
                return jnp.where(sel, s, NEG_INF), jnp.concatenate([far_row] * nb, axis=1)
            rows = []
            for a in range(nb):
                rows.append(jnp.concatenate(
                    [bias_ref[g, jnp.minimum(1 + nb * (qi - j) - a + bb, N_BIAS_OFFSETS - 1)]
                     for bb in range(nb)], axis=1))
            return jnp.where(sel, s + jnp.concatenate(rows, axis=0), NEG_INF), None

        return _run_chains(ms, s0_ref, acc_ref, lambda g: qk(g, j), lambda g: vt_ref[0, g, j],
                           adjust=adjust, guard=True, prefetch=lambda: qk(0, jnp.minimum(j + 1, qi)))

    far_dist = -(-(N_BIAS_OFFSETS + nb - 3) // nb)
    n_far = jnp.maximum(qi - far_dist + 1, 0)
    s0_ref[...] = qk(0, 0)
    ms = lax.fori_loop(0, n_far, lambda j, ms: step(j, ms, True), _chains_init(acc_ref, tq))
    lax.fori_loop(n_far, qi + 1, lambda j, ms: step(j, ms, False), ms)
    _chains_finish(acc_ref, o_ref)


def _dsa_attention(proj3, q_blk0, k_blk0, vt, mask, bias):
    b, s, _ = proj3.shape
    h = N_HEADS
    tq, g = ATT_TILE, ATT_HEADS
    assert q_blk0 % g == 0 and k_blk0 % g == 0 and h % g == 0
    once = pl.Buffered(1)
    return pl.pallas_call(
        _dsa_attn_kernel,
        grid=(b, h // g, s // tq),
        in_specs=[pl.BlockSpec((1, tq, g * HEAD_DIM), lambda bi, hi, qi: (bi, qi, q_blk0 // g + hi)),
                  pl.BlockSpec((1, s, g * HEAD_DIM), lambda bi, hi, qi: (bi, 0, k_blk0 // g + hi),
                               pipeline_mode=once),
                  pl.BlockSpec((1, g, s // tq, ATT_VROWS, tq), lambda bi, hi, qi: (bi, hi, 0, 0, 0),
                               pipeline_mode=once),
                  pl.BlockSpec((1, s, tq), lambda bi, hi, qi: (bi, 0, qi)),
                  pl.BlockSpec((g, N_BIAS_OFFSETS, BIAS_TILE, BIAS_TILE),
                               lambda bi, hi, qi: (hi, 0, 0, 0), pipeline_mode=once)],
        out_specs=pl.BlockSpec((1, tq, g * HEAD_DIM), lambda bi, hi, qi: (bi, qi, hi)),
        out_shape=jax.ShapeDtypeStruct((b, s, h * HEAD_DIM), BF16),
        scratch_shapes=[pltpu.VMEM((g, ATT_VROWS, tq), F32), pltpu.VMEM((tq, tq), F32)],
        compiler_params=_cparams(("parallel", "parallel", "arbitrary")),
        name="dsa_masked_attention",
    )(proj3, proj3, vt, mask, bias)


def _fox_attn_kernel(q_ref, k_ref, vt_ref, o_ref, acc_ref, s0_ref):
    tq = tk = ATT_TILE
    qi = pl.program_id(2)

    def qk(g, j):
        row0 = pl.multiple_of(j * tk, tk)
        cols = slice(g * ATT_KDIM, (g + 1) * ATT_KDIM)
        return _qk(k_ref[0, pl.ds(row0, tk), cols], q_ref[0, :, cols])

    def body(j, ms):
        return _run_chains(ms, s0_ref, acc_ref, lambda g: qk(g, j), lambda g: vt_ref[0, g, j],
                           guard=False, prefetch=lambda: qk(0, j + 1))

    s0_ref[...] = qk(0, 0)
    ms = lax.fori_loop(0, qi, body, _chains_init(acc_ref, tq))
    row = lax.broadcasted_iota(jnp.int32, (tk, tq), 0)
    col = lax.broadcasted_iota(jnp.int32, (tk, tq), 1)
    visible = row <= col
    _run_chains(ms, s0_ref, acc_ref, lambda g: qk(g, qi), lambda g: vt_ref[0, g, qi],
                adjust=lambda g, s: (jnp.where(visible, s, NEG_INF), None), guard=False, prefetch=None)
    _chains_finish(acc_ref, o_ref)


def _fox_attention(q_aug, k_aug, vt):
    b, s, _ = q_aug.shape
    h = N_HEADS
    tq, g = ATT_TILE, ATT_HEADS
    once = pl.Buffered(1)
    return pl.pallas_call(
        _fox_attn_kernel,
        grid=(b, h // g, s // tq),
        in_specs=[pl.BlockSpec((1, tq, g * ATT_KDIM), lambda bi, hi, qi: (bi, qi, hi)),
                  pl.BlockSpec((1, s, g * ATT_KDIM), lambda bi, hi, qi: (bi, 0, hi), pipeline_mode=once),
                  pl.BlockSpec((1, g, s // tq, ATT_VROWS, tq), lambda bi, hi, qi: (bi, hi, 0, 0, 0),
                               pipeline_mode=once)],
        out_specs=pl.BlockSpec((1, tq, g * HEAD_DIM), lambda bi, hi, qi: (bi, qi, hi)),
        out_shape=jax.ShapeDtypeStruct((b, s, h * HEAD_DIM), BF16),
        scratch_shapes=[pltpu.VMEM((g, ATT_VROWS, tq), F32), pltpu.VMEM((tq, tq), F32)],
        compiler_params=_cparams(("parallel", "parallel", "arbitrary")),
        name="fox_attention",
    )(q_aug, k_aug, vt)


def _t5_bucket(rel):
    half = N_BUCKETS // 2
    max_exact = half // 2
    base = jnp.where(rel > 0, half, 0)
    n = jnp.abs(rel)
    nf = jnp.maximum(n, max_exact).astype(jnp.float32)
    large = max_exact + (jnp.log(nf / max_exact) / math.log(MAX_DISTANCE / max_exact)
                         * (half - max_exact)).astype(jnp.int32)
    large = jnp.minimum(large, half - 1)
    return base + jnp.where(n < max_exact, n, large)


def _bias_tiles(rel_bias):
    t = BIAS_TILE
    half, max_exact = N_BUCKETS // 2, N_BUCKETS // 4
    last_bucket_start = max_exact * (MAX_DISTANCE / max_exact) ** ((half - 1 - max_exact) / (half - max_exact))
    assert (N_BIAS_OFFSETS - 2) * t - (t - 1) > last_bucket_start + 1
    i = jnp.arange(t, dtype=jnp.int32)[:, None]
    j = jnp.arange(t, dtype=jnp.int32)[None, :]
    rel = jnp.stack([i - j + (1 - o) * t for o in range(N_BIAS_OFFSETS)])
    onehot = jax.nn.one_hot(_t5_bucket(rel), N_BUCKETS, dtype=F32)
    return jnp.einsum("oijk,kh->hoij", onehot, rel_bias.astype(F32) * LOG2E,
                      precision=lax.Precision.HIGHEST)


def _values_t_tiled(a):
    b, s, _ = a.shape
    t = ATT_TILE
    vt = jnp.transpose(a.reshape(b, s // t, t, N_HEADS, HEAD_DIM), (0, 3, 1, 4, 2))
    ones = jnp.ones((b, N_HEADS, s // t, ATT_VROWS - HEAD_DIM, t), a.dtype)
    return jnp.concatenate([vt, ones], axis=3)


def _augment(a, extra):
    b, s, _ = a.shape
    lane = lax.broadcasted_iota(jnp.int32, (1, 1, 1, ATT_KDIM - HEAD_DIM), 3)
    tail = jnp.zeros((b, s, N_HEADS, ATT_KDIM - HEAD_DIM), a.dtype)
    for i in range(extra.shape[-1]):
        tail = jnp.where(lane == i, extra[..., i:i + 1], tail)
    out = jnp.concatenate([a.reshape(b, s, N_HEADS, HEAD_DIM), tail], axis=-1)
    return out.reshape(b, s, N_HEADS * ATT_KDIM)


def _round_bf16(x):
    return lax.reduce_precision(x, exponent_bits=8, mantissa_bits=7)


def _split3(c):
    hi = _round_bf16(c)
    r = c - hi
    mid = _round_bf16(r)
    lo = r - mid
    return hi.astype(BF16), mid.astype(BF16), lo.astype(BF16)


def kernel(x, w_in, b_f, w_out_a, w_out_b, w_o, w_up, w_down, g_mix, g_mlp, g_final, rel_bias):
    b, s, d = x.shape
    m = b * s
    depth = w_in.shape[0]
    assert s % ATT_TILE == 0 and ATT_TILE % BIAS_TILE == 0 and IDX_Q_TILE % CHUNK == 0
    top_k = min(TOPK_MAX, s // 4)

    sizes = (D_HEADS, D_HEADS, D_HEADS, N_IDX_HEADS * IDX_DIM, IDX_DIM, N_IDX_HEADS,
             D_HEADS, D_HEADS, D_HEADS, N_HEADS, d, d)
    offs = [0]
    for sz in sizes:
        offs.append(offs[-1] + sz)
    (o_qa, o_ka, o_va, o_iq, o_ik, o_iw, o_qb, o_kb, o_vb, o_fl, o_ga, o_gb, _) = offs

    p_qa, p_ka, p_va = 0, D_HEADS, 2 * D_HEADS
    p_iq = 3 * D_HEADS
    p_qb = p_iq + N_IDX_HEADS * IDX_DIM
    p_kb, p_vb = p_qb + D_HEADS, p_qb + 2 * D_HEADS
    p_ga = p_qb + 3 * D_HEADS
    p_gb = p_ga + d
    n_proj = p_gb + d

    q_scale = HEAD_DIM ** -0.5 * LOG2E
    bias = _bias_tiles(rel_bias)

    x2 = x.reshape(m, d)
    for l in range(depth):
        w = w_in[l]

        def cols(o, n):
            return lax.slice_in_dim(w, o, o + n, axis=1)

        w_main = jnp.concatenate([
            cols(o_qa, D_HEADS) * q_scale, cols(o_ka, 2 * D_HEADS),
            cols(o_iq, N_IDX_HEADS * IDX_DIM),
            cols(o_qb, D_HEADS) * q_scale, cols(o_kb, 2 * D_HEADS),
            cols(o_ga, 2 * d)], axis=1).astype(BF16)
        w_small = jnp.concatenate([
            cols(o_ik, IDX_DIM) * (IDX_DIM ** -0.5),
            cols(o_iw, N_IDX_HEADS) * (N_IDX_HEADS ** -0.5),
            cols(o_fl, N_HEADS),
            jnp.zeros((d, SMALL_W - IDX_DIM - N_IDX_HEADS - N_HEADS), F32)], axis=1)

        xn, small = _rmsnorm_small(x2, g_mix[l], w_small)
        proj = _matmul(xn, w_main, out_dtype=BF16, name="in_proj")
        proj3 = proj.reshape(b, s, n_proj)

        tq = IDX_Q_TILE
        n_q = s // tq
        ik = small[:, :IDX_DIM].astype(BF16).reshape(b, s, IDX_DIM)
        iq = lax.slice_in_dim(proj3, p_iq, p_iq + N_IDX_HEADS * IDX_DIM, axis=2)
        iq_t = jnp.transpose(iq.reshape(b, n_q, tq, N_IDX_HEADS, IDX_DIM), (0, 1, 4, 3, 2))
        iq_t = iq_t.reshape(b, n_q, IDX_DIM, N_IDX_HEADS * tq)
        iw = small[:, IDX_DIM:IDX_DIM + N_IDX_HEADS].reshape(b, n_q, tq, N_IDX_HEADS)
        iw_row = jnp.transpose(iw, (0, 1, 3, 2)).reshape(b, n_q, 1, N_IDX_HEADS * tq)
        mask = _indexer_mask(ik, iq_t, iw_row, top_k)

        va_t = _values_t_tiled(lax.slice_in_dim(proj3, p_va, p_va + D_HEADS, axis=2))
        ya = _dsa_attention(proj3, p_qa // HEAD_DIM, p_ka // HEAD_DIM, va_t, mask, bias)

        fl = small[:, IDX_DIM + N_IDX_HEADS:IDX_DIM + N_IDX_HEADS + N_HEADS].reshape(b, s, N_HEADS)
        log_f = jax.nn.log_sigmoid(fl + b_f[l].astype(F32))
        cum = jnp.cumsum(log_f, axis=1) * LOG2E
        c_hi, c_mid, c_lo = _split3(cum)
        ones = jnp.ones((b, s, N_HEADS), BF16)
        k_aug = _augment(lax.slice_in_dim(proj3, p_kb, p_kb + D_HEADS, axis=2),
                         jnp.stack([c_hi, c_mid, c_lo, ones, ones, ones], axis=-1))
        q_aug = _augment(lax.slice_in_dim(proj3, p_qb, p_qb + D_HEADS, axis=2),
                         jnp.stack([-ones, -ones, -ones, c_hi, c_mid, c_lo], axis=-1))
        vb_t = _values_t_tiled(lax.slice_in_dim(proj3, p_vb, p_vb + D_HEADS, axis=2))
        yb = _fox_attention(q_aug, k_aug, vb_t)

        merged = _gated_merge(ya.reshape(m, D_HEADS), yb.reshape(m, D_HEADS),
                              w_out_a[l].astype(BF16), w_out_b[l].astype(BF16),
                              proj, p_ga, p_gb, d)
        x2 = _matmul(merged, w_o[l].astype(BF16), out_dtype=F32, epilogue="residual",
                     residual=x2, tn=512, name="out_proj_residual")

        hn = _rmsnorm(x2, g_mlp[l], BF16)
        hid = _matmul(hn, w_up[l].astype(BF16), out_dtype=BF16, epilogue="relu2", name="mlp_up_relu2")
        x2 = _matmul(hid, w_down[l].astype(BF16), out_dtype=F32, epilogue="residual",
                     residual=x2, tk=2048, name="mlp_down_residual")

    return _rmsnorm(x2, g_final, F32).reshape(b, s, d)
```
